```python
import math
import jax, jax.numpy as jnp
from jax import lax
import numpy as np


D_MODEL = 2048
BATCH = 2
SEQ = 8192
DEPTH = 4

DA_HEADS = 4
DA_QK_DIM = 64
DA_V_DIM = 2 * DA_QK_DIM
DIL_PAIRS = ((128, 1), (512, 4), (2048, 16))
N_GROUPS = len(DIL_PAIRS)
DIL_HEADS = 4
DIL_HEAD_DIM = 128
CONV_CH = 1024
CONV_WIDTH = 3
A_WIDTH = DA_HEADS * DA_V_DIM
B_WIDTH = DIL_HEADS * DIL_HEAD_DIM
C_WIDTH = CONV_CH
MIX_WIDTH = A_WIDTH + B_WIDTH + C_WIDTH
N_BRANCH = 3
A_Q = DA_HEADS * 2 * DA_QK_DIM
A_K = DA_HEADS * 2 * DA_QK_DIM
A_V = A_WIDTH
B_QKV = N_GROUPS * 3 * B_WIDTH
C_IN = 3 * CONV_CH
IN_COLS = A_Q + A_K + A_V + B_QKV + C_IN
IN_SPLITS = (A_Q, A_Q + A_K, A_Q + A_K + A_V, A_Q + A_K + A_V + B_QKV)
MEM_LEN = 256
X_HEADS = 4
X_HEAD_DIM = 128
N_EXPERTS = 16
CAPACITY_FACTOR = 2
D_EXPERT = 1024
ROPE_THETA = 10000.0
EPS = 1e-6
QBLOCK = 128
MAX_POS_OFFSET = 1024

kernel_name = 'hybrid_gated_diff_dilated_conv_ec_moe_encoder'


def rms_norm(x, g):
    xf = x.astype(jnp.float32)
    y = xf * lax.rsqrt(jnp.mean(xf * xf, axis=-1, keepdims=True) + EPS)
    return (y * g.astype(jnp.float32)).astype(x.dtype)


def rope_tables(positions, dim, dtype):
    inv = ROPE_THETA ** (-jnp.arange(0, dim, 2, dtype=jnp.float32) / dim)
    ang = positions.astype(jnp.float32)[..., None] * inv
    return jnp.cos(ang).astype(dtype), jnp.sin(ang).astype(dtype)


def apply_rope(x, cos, sin):
    extra = x.ndim - 3
    c = cos.reshape(cos.shape[:2] + (1,) * extra + cos.shape[-1:])
    s = sin.reshape(sin.shape[:2] + (1,) * extra + sin.shape[-1:])
    x1, x2 = jnp.split(x, 2, axis=-1)
    return jnp.concatenate([x1 * c - x2 * s, x2 * c + x1 * s], axis=-1)


def diff_attention(q, k, v, lam, sub_gain, lam_init):
    B, S, H, _, d = q.shape
    nb = S // QBLOCK
    qb = q.reshape(B, nb, QBLOCK, H, 2, d).swapaxes(0, 1)
    scale = d ** -0.5

    def block(qblk):
        s = jnp.einsum('bqhmd,bkhmd->bhmqk', qblk, k).astype(jnp.float32) * scale
        p = jax.nn.softmax(s, axis=-1)
        a = p[:, :, 0] - lam * p[:, :, 1]
        return jnp.einsum('bhqk,bkhe->bqhe', a.astype(v.dtype), v)

    o = lax.map(block, qb)
    o = o.swapaxes(0, 1).reshape(B, S, H, -1)
    o = rms_norm(o, sub_gain) * (1.0 - lam_init)
    return o.reshape(B, S, H * o.shape[-1])


def dilated_window_attention(q, k, v, window, dilation):
    B, S, H, D = q.shape
    n_side = (window // 2) // dilation
    offs = dilation * jnp.arange(-n_side, n_side + 1, dtype=jnp.int32)
    nb = S // QBLOCK
    qb = q.reshape(B, nb, QBLOCK, H, D).swapaxes(0, 1)
    starts = jnp.arange(nb, dtype=jnp.int32) * QBLOCK
    scale = D ** -0.5

    def block(args):
        qblk, start = args
        idx = start + jnp.arange(QBLOCK, dtype=jnp.int32)[:, None] + offs[None, :]
        valid = (idx >= 0) & (idx < S)
        idxc = jnp.clip(idx, 0, S - 1)
        kg = k[:, idxc]
        vg = v[:, idxc]
        s = jnp.einsum('bqhd,bqjhd->bqhj', qblk, kg).astype(jnp.float32) * scale
        s = jnp.where(valid[None, :, None, :], s, -jnp.inf)
        m = jnp.max(s, axis=-1, keepdims=True)
        e = jnp.exp(s - m)
        l = jnp.sum(e, axis=-1, keepdims=True)
        o = jnp.einsum('bqhj,bqjhd->bqhd', (e / l).astype(v.dtype), vg)
        lse = (m + jnp.log(l))[..., 0]
        return o, lse

    o, lse = lax.map(block, (qb, starts))
    o = o.swapaxes(0, 1).reshape(B, S, H, D)
    lse = lse.swapaxes(0, 1).reshape(B, S, H)
    return o, lse


def short_conv(u, w):
    C = u.shape[-1]
    return lax.conv_general_dilated(u, w[:, None, :].astype(u.dtype), window_strides=(1,),
                                    padding=[(CONV_WIDTH // 2, CONV_WIDTH // 2)],
                                    dimension_numbers=('NWC', 'WIO', 'NWC'),
                                    feature_group_count=C)


def hybrid_mixer(xn, cos_a, sin_a, cos_b, sin_b, w_in, lam_p, sub_gain, conv_w,
                 w_br, w_gate, b_gate, w_out, lam_init):
    B, S, D = xn.shape
    proj = xn @ w_in
    a_q, a_k, a_v, b_qkv, c_in = jnp.split(proj, IN_SPLITS, axis=-1)

    qa = apply_rope(a_q.reshape(B, S, DA_HEADS, 2, DA_QK_DIM), cos_a, sin_a)
    ka = apply_rope(a_k.reshape(B, S, DA_HEADS, 2, DA_QK_DIM), cos_a, sin_a)
    va = a_v.reshape(B, S, DA_HEADS, DA_V_DIM)
    lp = lam_p.astype(jnp.float32)
    lam = jnp.exp(jnp.sum(lp[0] * lp[1])) - jnp.exp(jnp.sum(lp[2] * lp[3])) + lam_init
    y_a = diff_attention(qa, ka, va, lam, sub_gain, lam_init)

    bq = b_qkv.reshape(B, S, N_GROUPS, 3, DIL_HEADS, DIL_HEAD_DIM)
    outs, lses = [], []
    for g, (window, dilation) in enumerate(DIL_PAIRS):
        qg = apply_rope(bq[:, :, g, 0], cos_b, sin_b)
        kg = apply_rope(bq[:, :, g, 1], cos_b, sin_b)
        o, lse = dilated_window_attention(qg, kg, bq[:, :, g, 2], window, dilation)
        outs.append(o)
        lses.append(lse)
    wgt = jax.nn.softmax(jnp.stack(lses, axis=0), axis=0)
    y_b = jnp.einsum('gbsh,gbshd->bshd', wgt.astype(xn.dtype), jnp.stack(outs, axis=0))
    y_b = y_b.reshape(B, S, B_WIDTH)

    h, gb, gc = jnp.split(c_in, 3, axis=-1)
    y_c = gb * short_conv(gc * h, conv_w)

    p_a = y_a @ w_br[:A_WIDTH]
    p_b = y_b @ w_br[A_WIDTH:A_WIDTH + B_WIDTH]
    p_c = y_c @ w_br[A_WIDTH + B_WIDTH:]
    gates = jax.nn.sigmoid(xn @ w_gate + b_gate).reshape(B, S, N_BRANCH, D)
    merged = gates[:, :, 0] * p_a + gates[:, :, 1] * p_b + gates[:, :, 2] * p_c
    return merged @ w_out


def memory_cross_attention(xn, mem_n, wq, wkv, wo):
    B, S, _ = xn.shape
    M = mem_n.shape[1]
    q = (xn @ wq).reshape(B, S, X_HEADS, X_HEAD_DIM)
    k, v = jnp.split(mem_n @ wkv, 2, axis=-1)
    k = k.reshape(B, M, X_HEADS, X_HEAD_DIM)
    v = v.reshape(B, M, X_HEADS, X_HEAD_DIM)
    s = jnp.einsum('bshd,bmhd->bhsm', q, k).astype(jnp.float32) * (X_HEAD_DIM ** -0.5)
    p = jax.nn.softmax(s, axis=-1)
    o = jnp.einsum('bhsm,bmhd->bshd', p.astype(v.dtype), v).reshape(B, S, X_HEADS * X_HEAD_DIM)
    return o @ wo


def expert_choice_moe(xn, w_router, w1, w3, w2):
    B, S, D = xn.shape
    cap = CAPACITY_FACTOR * S // N_EXPERTS
    aff = jax.nn.softmax((xn @ w_router).astype(jnp.float32), axis=-1)
    gate, idx = lax.top_k(aff.swapaxes(1, 2), cap)
    bidx = jnp.arange(B)[:, None, None]
    xg = xn[bidx, idx]
    hdn = jax.nn.silu(jnp.einsum('becd,edf->becf', xg, w1)) * jnp.einsum('becd,edf->becf', xg, w3)
    y = jnp.einsum('becf,efd->becd', hdn, w2) * gate[..., None].astype(xn.dtype)
    return jnp.zeros_like(xn).at[bidx, idx].add(y)


def setup_inputs(seed: int = 0) -> dict:
    key = jax.random.key(seed)
    ks = jax.random.split(key, 24)
    f32 = jnp.float32
    L, D = DEPTH, D_MODEL

    def nrm(k, shape, scale):
        return jax.random.normal(k, shape, f32) * scale

    def gain(k, shape):
        return 1.0 + 0.02 * jax.random.normal(k, shape, f32)

    x = nrm(ks[0], (BATCH, SEQ, D), 1.0)
    mem = nrm(ks[1], (BATCH, MEM_LEN, D), 1.0)
    positions = (jnp.arange(SEQ, dtype=jnp.int32)[None, :]
                 + jax.random.randint(ks[2], (BATCH, 1), 0, MAX_POS_OFFSET, dtype=jnp.int32))
    branch_scale = jnp.concatenate([jnp.full((A_WIDTH,), A_WIDTH ** -0.5, f32),
                                    jnp.full((B_WIDTH,), B_WIDTH ** -0.5, f32),
                                    jnp.full((C_WIDTH,), C_WIDTH ** -0.5, f32)])
    return {
        'x': x,
        'mem': mem,
        'positions': positions,
        'norm_mix': gain(ks[3], (L, D)),
        'w_in': nrm(ks[4], (L, D, IN_COLS), D ** -0.5),
        'da_lambda': nrm(ks[5], (L, 4, DA_QK_DIM), 0.1),
        'da_subln': gain(ks[6], (L, DA_V_DIM)),
        'conv_w': nrm(ks[7], (L, CONV_WIDTH, CONV_CH), CONV_WIDTH ** -0.5),
        'w_branch': nrm(ks[8], (L, MIX_WIDTH, D), 1.0) * branch_scale[None, :, None],
        'w_gate': nrm(ks[9], (L, D, N_BRANCH * D), D ** -0.5),
        'b_gate': nrm(ks[10], (L, N_BRANCH * D), 0.02),
        'w_out': nrm(ks[11], (L, D, D), D ** -0.5),
        'norm_cross': gain(ks[12], (L, D)),
        'norm_mem': gain(ks[13], (L, D)),
        'wq_x': nrm(ks[14], (L, D, X_HEADS * X_HEAD_DIM), D ** -0.5),
        'wkv_x': nrm(ks[15], (L, D, 2 * X_HEADS * X_HEAD_DIM), D ** -0.5),
        'wo_x': nrm(ks[16], (L, X_HEADS * X_HEAD_DIM, D), (X_HEADS * X_HEAD_DIM) ** -0.5),
        'norm_moe': gain(ks[17], (L, D)),
        'w_router': nrm(ks[18], (L, D, N_EXPERTS), D ** -0.5),
        'w_e1': nrm(ks[19], (L, N_EXPERTS, D, D_EXPERT), D ** -0.5),
        'w_e3': nrm(ks[20], (L, N_EXPERTS, D, D_EXPERT), D ** -0.5),
        'w_e2': nrm(ks[21], (L, N_EXPERTS, D_EXPERT, D), D_EXPERT ** -0.5),
        'norm_final': gain(ks[22], (D,)),
    }


def reference(x, mem, positions, norm_mix, w_in, da_lambda, da_subln, conv_w, w_branch,
              w_gate, b_gate, w_out, norm_cross, norm_mem, wq_x, wkv_x, wo_x, norm_moe,
              w_router, w_e1, w_e3, w_e2, norm_final):
    cos_a, sin_a = rope_tables(positions, DA_QK_DIM, x.dtype)
    cos_b, sin_b = rope_tables(positions, DIL_HEAD_DIM, x.dtype)
    h = x
    for l in range(DEPTH):
        lam_init = 0.8 - 0.6 * math.exp(-0.3 * l)
        h = h + hybrid_mixer(rms_norm(h, norm_mix[l]), cos_a, sin_a, cos_b, sin_b,
                             w_in[l], da_lambda[l], da_subln[l], conv_w[l], w_branch[l],
                             w_gate[l], b_gate[l], w_out[l], lam_init)
        h = h + memory_cross_attention(rms_norm(h, norm_cross[l]), rms_norm(mem, norm_mem[l]),
                                       wq_x[l], wkv_x[l], wo_x[l])
        h = h + expert_choice_moe(rms_norm(h, norm_moe[l]), w_router[l], w_e1[l], w_e3[l], w_e2[l])
    return rms_norm(h, norm_final)
```

```python
import functools
import math

import jax
import jax.numpy as jnp
from jax import lax
from jax.experimental import pallas as pl
from jax.experimental.pallas import tpu as pltpu

F32 = jnp.float32
BF16 = jnp.bfloat16

EPS = 1e-6
ROPE_THETA = 10000.0
DA_HEADS = 4
DA_QK_DIM = 64
DA_V_DIM = 2 * DA_QK_DIM
DIL_PAIRS = ((128, 1), (512, 4), (2048, 16))
DIL_HEADS = 4
DIL_HEAD_DIM = 128
CONV_CH = 1024
A_WIDTH = DA_HEADS * DA_V_DIM
B_WIDTH = DIL_HEADS * DIL_HEAD_DIM
C_WIDTH = CONV_CH
MIX_WIDTH = A_WIDTH + B_WIDTH + C_WIDTH
N_BRANCH = 3
IN_COLS = 3 * A_WIDTH + len(DIL_PAIRS) * 3 * B_WIDTH + 3 * CONV_CH
X_HEADS = 4
X_HEAD_DIM = 128
CAPACITY_FACTOR = 2

LANES = 128
V7X_VMEM_BYTES = 64 * 1024 * 1024
VMEM_LIMIT = V7X_VMEM_BYTES - 8 * 1024 * 1024
COL_TILE = 512
QBLK = 128
NEG_BIG = -1e30


def _params(*sem):
    return pltpu.CompilerParams(dimension_semantics=sem, vmem_limit_bytes=VMEM_LIMIT)


def _rms(x, g):
    return x * lax.rsqrt(jnp.mean(x * x, axis=-1, keepdims=True) + EPS) * g


def _dot(a, b):
    return jnp.dot(a, b, preferred_element_type=F32)


def _dot_nt(a, b, **kw):
    return lax.dot_general(a, b, (((1,), (1,)), ((), ())), preferred_element_type=F32, **kw)


def _rms_kernel(x_ref, g_ref, o_ref):
    o_ref[...] = _rms(x_ref[...], g_ref[...]).astype(o_ref.dtype)


def rmsnorm(x, g, out_dtype):
    T, D = x.shape
    tm = min(512, T)
    return pl.pallas_call(
        _rms_kernel,
        grid=(T // tm,),
        in_specs=[pl.BlockSpec((tm, D), lambda i: (i, 0)),
                  pl.BlockSpec((1, D), lambda i: (0, 0))],
        out_specs=pl.BlockSpec((tm, D), lambda i: (i, 0)),
        out_shape=jax.ShapeDtypeStruct((T, D), out_dtype),
        compiler_params=_params("parallel"),
        name="rmsnorm",
    )(x, g.reshape(1, D))


N_COL_TILES = IN_COLS // COL_TILE
A_ROPE_TILES = 2 * A_WIDTH // COL_TILE
B_FIRST_TILE = 3 * A_WIDTH // COL_TILE
B_TILES_PER_GROUP = 3 * B_WIDTH // COL_TILE
B_ROPE_PER_GROUP = 2 * B_WIDTH // COL_TILE
C_FIRST_TILE = B_FIRST_TILE + len(DIL_PAIRS) * B_TILES_PER_GROUP


def _inproj_kernel(x_ref, w_ref, ca_ref, sa_ref, cb_ref, sb_ref, o_ref):
    j = pl.program_id(1)
    acc = _dot(x_ref[...], w_ref[...])
    jb = j - B_FIRST_TILE
    is_a = j < A_ROPE_TILES
    is_b = (j >= B_FIRST_TILE) & (j < C_FIRST_TILE) & (jb % B_TILES_PER_GROUP < B_ROPE_PER_GROUP)

    @pl.when(is_a)
    def _():
        c, s = ca_ref[...], sa_ref[...]
        lane = lax.broadcasted_iota(jnp.int32, c.shape, 1)
        first = (lane % DA_QK_DIM) < (DA_QK_DIM // 2)
        for t in range(COL_TILE // LANES):
            x = acc[:, t * LANES:(t + 1) * LANES]
            rot = jnp.where(first, pltpu.roll(x, LANES - DA_QK_DIM // 2, 1),
                            pltpu.roll(x, DA_QK_DIM // 2, 1))
            o_ref[:, t * LANES:(t + 1) * LANES] = (x * c + rot * s).astype(o_ref.dtype)

    @pl.when(is_b)
    def _():
        c, s = cb_ref[...], sb_ref[...]
        for t in range(COL_TILE // LANES):
            x = acc[:, t * LANES:(t + 1) * LANES]
            rot = pltpu.roll(x, DIL_HEAD_DIM // 2, 1)
            o_ref[:, t * LANES:(t + 1) * LANES] = (x * c + rot * s).astype(o_ref.dtype)

    @pl.when(jnp.logical_not(is_a | is_b))
    def _():
        o_ref[...] = acc.astype(o_ref.dtype)


def in_projection(xn, w, ca, sa, cb, sb):
    T, D = xn.shape
    tm = min(1024, T)
    tab = pl.BlockSpec((tm, LANES), lambda i, j: (i, 0))
    return pl.pallas_call(
        _inproj_kernel,
        grid=(T // tm, N_COL_TILES),
        in_specs=[pl.BlockSpec((tm, D), lambda i, j: (i, 0)),
                  pl.BlockSpec((D, COL_TILE), lambda i, j: (0, j)),
                  tab, tab, tab, tab],
        out_specs=pl.BlockSpec((tm, COL_TILE), lambda i, j: (i, j)),
        out_shape=jax.ShapeDtypeStruct((T, IN_COLS), BF16),
        compiler_params=_params("parallel", "arbitrary"),
        name="in_projection",
    )(xn, w, ca, sa, cb, sb)


def _diff_attn_kernel(q_ref, k_ref, v_ref, lam_ref, li_ref, g_ref, o_ref, *, tk):
    S = k_ref.shape[1]
    tq = q_ref.shape[1]
    q = q_ref[0] * jnp.asarray(DA_QK_DIM ** -0.5, BF16)
    lane = lax.broadcasted_iota(jnp.int32, q.shape, 1)
    zero = jnp.zeros_like(q)
    q0 = jnp.where(lane < DA_QK_DIM, q, zero)
    q1 = jnp.where(lane >= DA_QK_DIM, q, zero)

    def step(qm, kj, vj, m, l, a):
        s = _dot_nt(qm, kj)
        m_new = jnp.maximum(m, jnp.max(s, axis=-1, keepdims=True))
        p = jnp.exp(s - m_new)
        alpha = jnp.exp(m - m_new)
        l = alpha * l + jnp.sum(p, axis=-1, keepdims=True)
        a = alpha * a + _dot(p.astype(BF16), vj)
        return m_new, l, a

    def body(j, carry):
        m0, l0, a0, m1, l1, a1 = carry
        off = pl.multiple_of(j * tk, tk)
        kj = k_ref[0, pl.ds(off, tk), :]
        vj = v_ref[0, pl.ds(off, tk), :]
        m0, l0, a0 = step(q0, kj, vj, m0, l0, a0)
        m1, l1, a1 = step(q1, kj, vj, m1, l1, a1)
        return m0, l0, a0, m1, l1, a1

    mi = jnp.full((tq, 1), NEG_BIG, F32)
    li = jnp.zeros((tq, 1), F32)
    ai = jnp.zeros((tq, DA_V_DIM), F32)
    m0, l0, a0, m1, l1, a1 = lax.fori_loop(0, S // tk, body, (mi, li, ai, mi, li, ai))

    lp = lam_ref[...]
    lam_init = li_ref[...]
    lam = (jnp.exp(jnp.sum(lp[0:1] * lp[1:2], axis=-1, keepdims=True))
           - jnp.exp(jnp.sum(lp[2:3] * lp[3:4], axis=-1, keepdims=True)) + lam_init)
    o = a0 / l0 - lam * (a1 / l1)
    o_ref[0] = (_rms(o, g_ref[...]) * (1.0 - lam_init)).astype(o_ref.dtype)


def diff_attention(proj3, lam_p, lam_init, sub_gain):
    B, S, _ = proj3.shape
    tq = min(512, S)
    tk = min(512, S)
    kb = A_WIDTH // DA_V_DIM
    return pl.pallas_call(
        functools.partial(_diff_attn_kernel, tk=tk),
        grid=(B, DA_HEADS, S // tq),
        in_specs=[pl.BlockSpec((1, tq, DA_V_DIM), lambda b, h, i: (b, i, h)),
                  pl.BlockSpec((1, S, DA_V_DIM), lambda b, h, i: (b, 0, kb + h)),
                  pl.BlockSpec((1, S, DA_V_DIM), lambda b, h, i: (b, 0, 2 * kb + h)),
                  pl.BlockSpec((4, DA_QK_DIM), lambda b, h, i: (0, 0)),
                  pl.BlockSpec((1, 1), lambda b, h, i: (0, 0)),
                  pl.BlockSpec((1, DA_V_DIM), lambda b, h, i: (0, 0))],
        out_specs=pl.BlockSpec((1, tq, DA_V_DIM), lambda b, h, i: (b, i, h)),
        out_shape=jax.ShapeDtypeStruct((B, S, A_WIDTH), BF16),
        compiler_params=_params("parallel", "parallel", "arbitrary"),
        name="diff_attention",
    )(proj3, proj3, proj3, lam_p, lam_init, sub_gain.reshape(1, DA_V_DIM))


def _dilated_kernel(q_ref, kp_ref, kc_ref, kn_ref, vp_ref, vc_ref, vn_ref, o_ref, lse_ref, *,
                    n_side):
    i = pl.program_id(2)
    n_blk = pl.num_programs(2)
    rows = lax.broadcasted_iota(jnp.int32, (QBLK, 3 * QBLK), 0)
    cols = lax.broadcasted_iota(jnp.int32, (QBLK, 3 * QBLK), 1)
    rel = cols - QBLK - rows
    kpos = (i - 1) * QBLK + cols
    valid = (jnp.abs(rel) <= n_side) & (kpos >= 0) & (kpos < n_blk * QBLK)
    scale = DIL_HEAD_DIM ** -0.5
    for h in range(DIL_HEADS):
        hs = slice(h * DIL_HEAD_DIM, (h + 1) * DIL_HEAD_DIM)
        kcat = jnp.concatenate([kp_ref[0, :, hs], kc_ref[0, :, hs], kn_ref[0, :, hs]], axis=0)
        vcat = jnp.concatenate([vp_ref[0, :, hs], vc_ref[0, :, hs], vn_ref[0, :, hs]], axis=0)
        s = jnp.where(valid, _dot_nt(q_ref[0, :, hs], kcat) * scale, NEG_BIG)
        m = jnp.max(s, axis=-1, keepdims=True)
        e = jnp.exp(s - m)
        l = jnp.sum(e, axis=-1, keepdims=True)
        o_ref[0, :, hs] = _dot((e / l).astype(BF16), vcat).astype(o_ref.dtype)
        lse_ref[0, :, hs] = jnp.broadcast_to(m + jnp.log(l), (QBLK, DIL_HEAD_DIM))


def dilated_attention(proj3, group, window, dilation):
    B, S, _ = proj3.shape
    n_side = (window // 2) // dilation
    assert n_side <= QBLK and S % (dilation * QBLK) == 0
    L = S // dilation
    nb = L // QBLK
    tiles = IN_COLS // B_WIDTH
    pv = proj3.reshape(B, L, dilation * IN_COLS)
    base = B_FIRST_TILE + group * B_TILES_PER_GROUP

    def spec(kind, shift):
        def imap(b, r, i):
            return (b, jnp.clip(i + shift, 0, nb - 1), r * tiles + base + kind)
        return pl.BlockSpec((1, QBLK, B_WIDTH), imap)

    out_spec = pl.BlockSpec((1, QBLK, B_WIDTH), lambda b, r, i: (b, i, r))
    o, lse = pl.pallas_call(
        functools.partial(_dilated_kernel, n_side=n_side),
        grid=(B, dilation, nb),
        in_specs=[spec(0, 0), spec(1, -1), spec(1, 0), spec(1, 1),
                  spec(2, -1), spec(2, 0), spec(2, 1)],
        out_specs=[out_spec, out_spec],
        out_shape=[jax.ShapeDtypeStruct((B, L, dilation * B_WIDTH), BF16),
                   jax.ShapeDtypeStruct((B, L, dilation * B_WIDTH), F32)],
        compiler_params=_params("parallel", "parallel", "arbitrary"),
        name=f"dilated_attention_g{group}",
    )(pv, pv, pv, pv, pv, pv, pv)
    return o.reshape(B * S, B_WIDTH), lse.reshape(B * S, B_WIDTH)


def _merge_kernel(o0_ref, o1_ref, o2_ref, l0_ref, l1_ref, l2_ref, y_ref):
    l0, l1, l2 = l0_ref[...], l1_ref[...], l2_ref[...]
    m = jnp.maximum(jnp.maximum(l0, l1), l2)
    w0, w1, w2 = jnp.exp(l0 - m), jnp.exp(l1 - m), jnp.exp(l2 - m)
    num = w0 * o0_ref[...].astype(F32) + w1 * o1_ref[...].astype(F32) + w2 * o2_ref[...].astype(F32)
    y_ref[...] = (num / (w0 + w1 + w2)).astype(y_ref.dtype)


def merge_groups(outs, lses):
    T = outs[0].shape[0]
    tm = min(1024, T)
    spec = pl.BlockSpec((tm, B_WIDTH), lambda i: (i, 0))
    return pl.pallas_call(
        _merge_kernel,
        grid=(T // tm,),
        in_specs=[spec] * 6,
        out_specs=spec,
        out_shape=jax.ShapeDtypeStruct((T, B_WIDTH), BF16),
        compiler_params=_params("parallel"),
        name="merge_groups",
    )(*outs, *lses)


HALO_ROWS = 16


def _conv_kernel(h_ref, gb_ref, gc_ref, hp_ref, gp_ref, hn_ref, gn_ref, w_ref, o_ref):
    i = pl.program_id(1)
    ts = h_ref.shape[1]
    u = gc_ref[0].astype(F32) * h_ref[0].astype(F32)
    up = (gp_ref[0].astype(F32) * hp_ref[0].astype(F32))[HALO_ROWS - 1:HALO_ROWS]
    un = (gn_ref[0].astype(F32) * hn_ref[0].astype(F32))[0:1]
    up = jnp.where(i > 0, up, 0.0)
    un = jnp.where(i < pl.num_programs(1) - 1, un, 0.0)
    rows = lax.broadcasted_iota(jnp.int32, u.shape, 0)
    u_prev = jnp.where(rows == 0, up, pltpu.roll(u, 1, 0))
    u_next = jnp.where(rows == ts - 1, un, pltpu.roll(u, ts - 1, 0))
    w = w_ref[...]
    y = gb_ref[0].astype(F32) * (w[0:1] * u_prev + w[1:2] * u + w[2:3] * u_next)
    o_ref[0] = y.astype(o_ref.dtype)


def short_conv_branch(proj3, conv_w):
    B, S, _ = proj3.shape
    ts = min(512, S)
    c0 = (IN_COLS - 3 * CONV_CH) // CONV_CH
    hb = ts // HALO_ROWS
    n_halo = S // HALO_ROWS

    def main(c):
        return pl.BlockSpec((1, ts, CONV_CH), lambda b, i: (b, i, c))

    def prev(c):
        return pl.BlockSpec((1, HALO_ROWS, CONV_CH), lambda b, i: (b, jnp.maximum(i * hb - 1, 0), c))

    def nxt(c):
        return pl.BlockSpec((1, HALO_ROWS, CONV_CH),
                            lambda b, i: (b, jnp.minimum((i + 1) * hb, n_halo - 1), c))

    return pl.pallas_call(
        _conv_kernel,
        grid=(B, S // ts),
        in_specs=[main(c0), main(c0 + 1), main(c0 + 2), prev(c0), prev(c0 + 2), nxt(c0), nxt(c0 + 2),
                  pl.BlockSpec((3, CONV_CH), lambda b, i: (0, 0))],
        out_specs=pl.BlockSpec((1, ts, CONV_CH), lambda b, i: (b, i, 0)),
        out_shape=jax.ShapeDtypeStruct((B, S, CONV_CH), BF16),
        compiler_params=_params("parallel", "parallel"),
        name="short_conv",
    )(proj3, proj3, proj3, proj3, proj3, proj3, proj3, conv_w)


def _gate_merge_kernel(xn_ref, ya_ref, yb_ref, yc_ref, wg0_ref, wg1_ref, wg2_ref,
                       b0_ref, b1_ref, b2_ref, wa_ref, wb_ref, wc_ref, o_ref):
    xn = xn_ref[...]
    acc = jax.nn.sigmoid(_dot(xn, wg0_ref[...]) + b0_ref[...]) * _dot(ya_ref[...], wa_ref[...])
    acc += jax.nn.sigmoid(_dot(xn, wg1_ref[...]) + b1_ref[...]) * _dot(yb_ref[...], wb_ref[...])
    acc += jax.nn.sigmoid(_dot(xn, wg2_ref[...]) + b2_ref[...]) * _dot(yc_ref[...], wc_ref[...])
    o_ref[...] = acc.astype(o_ref.dtype)


def gate_merge(xn, ya, yb, yc, w_gate, b_gate, w_br):
    T, D = xn.shape
    tm = min(512, T)
    tn = min(512, D)
    nj = D // tn
    assert A_WIDTH == B_WIDTH and C_WIDTH == 2 * A_WIDTH

    def wg(k):
        return pl.BlockSpec((D, tn), lambda i, j: (0, k * nj + j))

    def bg(k):
        return pl.BlockSpec((1, tn), lambda i, j: (0, k * nj + j))

    def rowblk(w):
        return pl.BlockSpec((tm, w), lambda i, j: (i, 0))

    return pl.pallas_call(
        _gate_merge_kernel,
        grid=(T // tm, nj),
        in_specs=[rowblk(D), rowblk(A_WIDTH), rowblk(B_WIDTH), rowblk(C_WIDTH),
                  wg(0), wg(1), wg(2), bg(0), bg(1), bg(2),
                  pl.BlockSpec((A_WIDTH, tn), lambda i, j: (0, j)),
                  pl.BlockSpec((B_WIDTH, tn), lambda i, j: (1, j)),
                  pl.BlockSpec((C_WIDTH, tn), lambda i, j: (1, j))],
        out_specs=pl.BlockSpec((tm, tn), lambda i, j: (i, j)),
        out_shape=jax.ShapeDtypeStruct((T, D), BF16),
        compiler_params=_params("parallel", "arbitrary"),
        name="gate_merge",
    )(xn, ya, yb, yc, w_gate, w_gate, w_gate, b_gate, b_gate, b_gate, w_br, w_br, w_br)


def _matmul_residual_kernel(x_ref, w_ref, r_ref, o_ref):
    o_ref[...] = r_ref[...] + _dot(x_ref[...], w_ref[...])


def matmul_residual(x, w, res):
    T, K = x.shape
    N = w.shape[1]
    tm = min(1024, T)
    tn = min(1024, N)
    return pl.pallas_call(
        _matmul_residual_kernel,
        grid=(T // tm, N // tn),
        in_specs=[pl.BlockSpec((tm, K), lambda i, j: (i, 0)),
                  pl.BlockSpec((K, tn), lambda i, j: (0, j)),
                  pl.BlockSpec((tm, tn), lambda i, j: (i, j))],
        out_specs=pl.BlockSpec((tm, tn), lambda i, j: (i, j)),
        out_shape=jax.ShapeDtypeStruct((T, N), F32),
        input_output_aliases={2: 0},
        compiler_params=_params("parallel", "arbitrary"),
        name="matmul_residual",
    )(x, w, res)


def _memkv_kernel(mem_ref, g_ref, w_ref, o_ref):
    o_ref[0, 0] = _dot(_rms(mem_ref[0], g_ref[0]).astype(BF16), w_ref[0]).astype(o_ref.dtype)


def memory_kv(mem, norm_mem, wkv):
    B, M, D = mem.shape
    L, _, N = wkv.shape
    return pl.pallas_call(
        _memkv_kernel,
        grid=(L, B),
        in_specs=[pl.BlockSpec((1, M, D), lambda l, b: (b, 0, 0)),
                  pl.BlockSpec((1, 1, D), lambda l, b: (l, 0, 0)),
                  pl.BlockSpec((1, D, N), lambda l, b: (l, 0, 0))],
        out_specs=pl.BlockSpec((1, 1, M, N), lambda l, b: (l, b, 0, 0)),
        out_shape=jax.ShapeDtypeStruct((L, B, M, N), BF16),
        compiler_params=_params("parallel", "parallel"),
        name="memory_kv",
    )(mem, norm_mem.reshape(L, 1, D), wkv)


def _cross_kernel(h_ref, gx_ref, wq_ref, kv_ref, wo_ref, gm_ref, wr_ref, ho_ref, xn_ref, aff_ref):
    h = h_ref[0]
    q = _dot(_rms(h, gx_ref[...]).astype(BF16), wq_ref[...]) * (X_HEAD_DIM ** -0.5)
    q = q.astype(BF16)
    kv = kv_ref[0]
    outs = []
    for hd in range(X_HEADS):
        ks = slice(hd * X_HEAD_DIM, (hd + 1) * X_HEAD_DIM)
        vs = slice((X_HEADS + hd) * X_HEAD_DIM, (X_HEADS + hd + 1) * X_HEAD_DIM)
        s = _dot_nt(q[:, ks], kv[:, ks])
        e = jnp.exp(s - jnp.max(s, axis=-1, keepdims=True))
        p = e / jnp.sum(e, axis=-1, keepdims=True)
        outs.append(_dot(p.astype(BF16), kv[:, vs]))
    o = jnp.concatenate(outs, axis=1).astype(BF16)
    hn = h + _dot(o, wo_ref[...])
    ho_ref[0] = hn
    xn = _rms(hn, gm_ref[...])
    xn_ref[0] = xn
    lt = _dot_nt(wr_ref[...], xn, precision=lax.Precision.HIGHEST)
    e = jnp.exp(lt - jnp.max(lt, axis=0, keepdims=True))
    aff_ref[0] = e / jnp.sum(e, axis=0, keepdims=True)


def cross_attention_router(h3, g_cross, wq, kv, wo, g_moe, w_router_t):
    B, S, D = h3.shape
    M, NKV = kv.shape[1:]
    E = w_router_t.shape[0]
    NQ = wq.shape[1]
    tm = min(512, S)

    def full(shape):
        return pl.BlockSpec(shape, lambda b, i: (0,) * len(shape))

    return pl.pallas_call(
        _cross_kernel,
        grid=(B, S // tm),
        in_specs=[pl.BlockSpec((1, tm, D), lambda b, i: (b, i, 0)),
                  full((1, D)), full((D, NQ)),
                  pl.BlockSpec((1, M, NKV), lambda b, i: (b, 0, 0)),
                  full((NQ, D)), full((1, D)), full((E, D))],
        out_specs=[pl.BlockSpec((1, tm, D), lambda b, i: (b, i, 0)),
                   pl.BlockSpec((1, tm, D), lambda b, i: (b, i, 0)),
                   pl.BlockSpec((1, E, tm), lambda b, i: (b, 0, i))],
        out_shape=[jax.ShapeDtypeStruct((B, S, D), F32),
                   jax.ShapeDtypeStruct((B, S, D), F32),
                   jax.ShapeDtypeStruct((B, E, S), F32)],
        input_output_aliases={0: 0},
        compiler_params=_params("parallel", "parallel"),
        name="cross_attention_router",
    )(h3, g_cross.reshape(1, D), wq, kv, wo, g_moe.reshape(1, D), w_router_t)


SLOT_BLK = 64


def _select_kernel(aff_ref, idx_ref, gate_ref, cs_ref, w_ref, ts_ref, *, cap):
    E, C, _ = aff_ref.shape[1:]
    R = E * C
    a = aff_ref[0].reshape(R, LANES)
    bits = pltpu.bitcast(a, jnp.int32)
    rid = lax.broadcasted_iota(jnp.int32, (R, LANES), 0)

    ii = lax.broadcasted_iota(jnp.int32, (LANES, LANES), 0)
    jj = lax.broadcasted_iota(jnp.int32, (LANES, LANES), 1)
    upper = jnp.where(ii <= jj, 1.0, 0.0).astype(BF16)
    ones = jnp.ones((LANES, LANES), BF16)
    ri = lax.broadcasted_iota(jnp.int32, (R, R), 0) // C
    rj = lax.broadcasted_iota(jnp.int32, (R, R), 1)
    same_expert = jnp.where(rj // C == ri, 1.0, 0.0).astype(BF16)
    rows_before = jnp.where((rj // C == ri) & (rj < lax.broadcasted_iota(jnp.int32, (R, R), 0)),
                            1.0, 0.0).astype(BF16)

    def row_totals(x):
        return _dot(x.astype(BF16), ones).astype(BF16)

    def count(x):
        return _dot(same_expert, row_totals(x))

    def prefix_count(x):
        return _dot(x.astype(BF16), upper) + _dot(rows_before, row_totals(x))

    thr = jnp.zeros((R, LANES), jnp.int32)
    for bit in range(30, -1, -1):
        cand = thr | jnp.int32(1 << bit)
        thr = jnp.where(count(jnp.where(bits >= cand, 1.0, 0.0)) >= cap, cand, thr)
    gt_f = jnp.where(bits > thr, 1.0, 0.0)
    eq_f = jnp.where(bits == thr, 1.0, 0.0)

    need = cap - count(gt_f)
    eq_rank = prefix_count(eq_f) - eq_f
    sel = gt_f + eq_f * jnp.where(eq_rank < need, 1.0, 0.0)
    csum = prefix_count(sel)
    tok = ((rid % C) * LANES + lax.broadcasted_iota(jnp.int32, (R, LANES), 1)).astype(F32)
    cs_ref[...] = csum
    w_ref[...] = a * sel
    ts_ref[...] = tok * sel

    def per_expert(e, _):
        for sb in range(cap // SLOT_BLK):
            k1 = (lax.broadcasted_iota(jnp.int32, (SLOT_BLK, LANES), 0) + (sb * SLOT_BLK + 1)).astype(F32)

            def sweep(c, acc):
                ai, ag = acc
                r = e * C + c
                hit = cs_ref[pl.ds(r, 1), :] == k1
                ai = ai + jnp.where(hit, ts_ref[pl.ds(r, 1), :], 0.0)
                ag = ag + jnp.where(hit, w_ref[pl.ds(r, 1), :], 0.0)
                return ai, ag

            z = jnp.zeros((SLOT_BLK, LANES), F32)
            ai, ag = lax.fori_loop(0, C, sweep, (z, z))
            rows = pl.ds(sb * SLOT_BLK, SLOT_BLK)
            idx_ref[0, e, rows, :] = jnp.sum(ai, axis=1, keepdims=True).astype(jnp.int32)
            gate_ref[0, e, rows, :] = jnp.sum(ag, axis=1, keepdims=True)
        return 0

    lax.fori_loop(0, E, per_expert, 0)


def expert_choice_select(aff_t, cap):
    B, E, S = aff_t.shape
    C = S // LANES
    assert cap % SLOT_BLK == 0
    idx, gate = pl.pallas_call(
        functools.partial(_select_kernel, cap=cap),
        grid=(B,),
        in_specs=[pl.BlockSpec((1, E, C, LANES), lambda b: (b, 0, 0, 0))],
        out_specs=[pl.BlockSpec((1, E, cap, 1), lambda b: (b, 0, 0, 0)),
                   pl.BlockSpec((1, E, cap, 1), lambda b: (b, 0, 0, 0))],
        out_shape=[jax.ShapeDtypeStruct((B, E, cap, 1), jnp.int32),
                   jax.ShapeDtypeStruct((B, E, cap, 1), F32)],
        scratch_shapes=[pltpu.VMEM((E * C, LANES), F32)] * 3,
        compiler_params=_params("parallel"),
        name="expert_choice_select",
    )(aff_t.reshape(B, E, C, LANES))
    return idx, gate


def _expert_ffn_kernel(idx_hbm, gate_ref, xn_hbm, h_in_hbm, w1_ref, w3_ref, w2_ref, h_hbm,
                       idx_smem, xbuf, hbuf, sems, *, sub, seq_len):
    del h_in_hbm
    e = pl.program_id(0)
    b = pl.program_id(1)
    n_exp = pl.num_programs(0)
    cap = idx_smem.shape[1]
    cp = pltpu.make_async_copy(idx_hbm.at[pl.ds(b * n_exp + e, 1), :], idx_smem, sems.at[0])
    cp.start()
    cp.wait()
    base = b * seq_len

    def token_row(t, r):
        return pl.ds(base + idx_smem[0, t * sub + r], 1)

    def tile(t, _):
        def gather(r, _):
            src = token_row(t, r)
            pltpu.make_async_copy(xn_hbm.at[src, :], xbuf.at[pl.ds(r, 1), :], sems.at[1]).start()
            pltpu.make_async_copy(h_hbm.at[src, :], hbuf.at[pl.ds(r, 1), :], sems.at[2]).start()
            return 0

        lax.fori_loop(0, sub, gather, 0)
        pltpu.make_async_copy(xn_hbm.at[pl.ds(0, sub), :], xbuf, sems.at[1]).wait()
        pltpu.make_async_copy(h_hbm.at[pl.ds(0, sub), :], hbuf, sems.at[2]).wait()

        xg = xbuf[...].astype(BF16)
        a = _dot(xg, w1_ref[0])
        c = _dot(xg, w3_ref[0])
        hid = (a * jax.nn.sigmoid(a) * c).astype(BF16)
        y = _dot(hid, w2_ref[0])
        g = gate_ref[0, 0, pl.ds(pl.multiple_of(t * sub, sub), sub), :]
        hbuf[...] = hbuf[...] + y * g

        def scatter(r, _):
            pltpu.make_async_copy(hbuf.at[pl.ds(r, 1), :], h_hbm.at[token_row(t, r), :],
                                  sems.at[3]).start()
            return 0

        lax.fori_loop(0, sub, scatter, 0)
        pltpu.make_async_copy(hbuf, h_hbm.at[pl.ds(0, sub), :], sems.at[3]).wait()
        return 0

    lax.fori_loop(0, cap // sub, tile, 0)


def expert_ffn(idx, gate, xn, h, w1, w3, w2, seq_len):
    T, D = h.shape
    B, E, cap, _ = gate.shape
    F = w1.shape[-1]
    sub = min(256, cap)
    return pl.pallas_call(
        functools.partial(_expert_ffn_kernel, sub=sub, seq_len=seq_len),
        grid=(E, B),
        in_specs=[pl.BlockSpec(memory_space=pl.ANY),
                  pl.BlockSpec((1, 1, cap, 1), lambda e, b: (b, e, 0, 0)),
                  pl.BlockSpec(memory_space=pl.ANY),
                  pl.BlockSpec(memory_space=pl.ANY),
                  pl.BlockSpec((1, D, F), lambda e, b: (e, 0, 0)),
                  pl.BlockSpec((1, D, F), lambda e, b: (e, 0, 0)),
                  pl.BlockSpec((1, F, D), lambda e, b: (e, 0, 0))],
        out_specs=pl.BlockSpec(memory_space=pl.ANY),
        out_shape=jax.ShapeDtypeStruct((T, D), F32),
        scratch_shapes=[pltpu.SMEM((1, cap), jnp.int32),
                        pltpu.VMEM((sub, D), F32),
                        pltpu.VMEM((sub, D), F32),
                        pltpu.SemaphoreType.DMA((4,))],
        input_output_aliases={3: 0},
        compiler_params=_params("arbitrary", "arbitrary"),
        name="expert_ffn",
    )(idx.reshape(B * E, cap), gate, xn, h, w1, w3, w2)


def _rope_tables(positions, dim, reps):
    inv = ROPE_THETA ** (-jnp.arange(0, dim, 2, dtype=F32) / dim)
    ang = positions.astype(F32).reshape(-1, 1) * inv
    cos, sin = jnp.cos(ang), jnp.sin(ang)
    return (jnp.tile(jnp.concatenate([cos, cos], axis=-1), (1, reps)),
            jnp.tile(jnp.concatenate([-sin, sin], axis=-1), (1, reps)))


def kernel(x, mem, positions, norm_mix, w_in, da_lambda, da_subln, conv_w, w_branch, w_gate, b_gate,
           w_out, norm_cross, norm_mem, wq_x, wkv_x, wo_x, norm_moe, w_router, w_e1, w_e3, w_e2,
           norm_final):
    B, S, D = x.shape
    depth = w_in.shape[0]
    E = w_router.shape[-1]
    cap = CAPACITY_FACTOR * S // E
    T = B * S

    ca, sa = _rope_tables(positions, DA_QK_DIM, LANES // DA_QK_DIM)
    cb, sb = _rope_tables(positions, DIL_HEAD_DIM, 1)
    kv_all = memory_kv(mem, norm_mem, wkv_x.astype(BF16))

    h = x.reshape(T, D)
    for l in range(depth):
        lam_init = jnp.full((1, 1), 0.8 - 0.6 * math.exp(-0.3 * l), F32)
        xn = rmsnorm(h, norm_mix[l], BF16)
        proj = in_projection(xn, w_in[l].astype(BF16), ca, sa, cb, sb)
        proj3 = proj.reshape(B, S, IN_COLS)
        ya = diff_attention(proj3, da_lambda[l], lam_init, da_subln[l]).reshape(T, A_WIDTH)
        outs, lses = zip(*[dilated_attention(proj3, g, w, d) for g, (w, d) in enumerate(DIL_PAIRS)])
        yb = merge_groups(outs, lses)
        yc = short_conv_branch(proj3, conv_w[l]).reshape(T, C_WIDTH)
        merged = gate_merge(xn, ya, yb, yc, w_gate[l].astype(BF16), b_gate[l].reshape(1, -1),
                            w_branch[l].astype(BF16))
        h = matmul_residual(merged, w_out[l].astype(BF16), h)
        h3, xn3, aff_t = cross_attention_router(
            h.reshape(B, S, D), norm_cross[l], wq_x[l].astype(BF16), kv_all[l],
            wo_x[l].astype(BF16), norm_moe[l], w_router[l].T)
        idx, gate = expert_choice_select(aff_t, cap)
        h = expert_ffn(idx, gate, xn3.reshape(T, D), h3.reshape(T, D), w_e1[l].astype(BF16),
                       w_e3[l].astype(BF16), w_e2[l].astype(BF16), S)
    return rmsnorm(h, norm_final, F32).reshape(B, S, D)
```

```python
import functools
import math

import jax
import jax.numpy as jnp
from jax import lax
from jax.experimental import pallas as pl
from jax.experimental.pallas import tpu as pltpu

F32 = jnp.float32
BF16 = jnp.bfloat16

EPS = 1e-6
ROPE_THETA = 10000.0
DA_HEADS = 4
DA_QK_DIM = 64
DA_V_DIM = 2 * DA_QK_DIM
DIL_PAIRS = ((128, 1), (512, 4), (2048, 16))
DIL_HEADS = 4
DIL_HEAD_DIM = 128
CONV_CH = 1024
A_WIDTH = DA_HEADS * DA_V_DIM
B_WIDTH = DIL_HEADS * DIL_HEAD_DIM
C_WIDTH = CONV_CH
MIX_WIDTH = A_WIDTH + B_WIDTH + C_WIDTH
N_BRANCH = 3
IN_COLS = 3 * A_WIDTH + len(DIL_PAIRS) * 3 * B_WIDTH + 3 * CONV_CH
X_HEADS = 4
X_HEAD_DIM = 128
CAPACITY_FACTOR = 2

LANES = 128
V7X_VMEM_BYTES = 64 * 1024 * 1024
VMEM_LIMIT = V7X_VMEM_BYTES - 8 * 1024 * 1024
COL_TILE = 512
QBLK = 128
NEG_BIG = -1e30


def _params(*sem):
    return pltpu.CompilerParams(dimension_semantics=sem, vmem_limit_bytes=VMEM_LIMIT)


def _rms(x, g):
    return x * lax.rsqrt(jnp.mean(x * x, axis=-1, keepdims=True) + EPS) * g


def _dot(a, b):
    return jnp.dot(a, b, preferred_element_type=F32)


def _dot_nt(a, b, **kw):
    return lax.dot_general(a, b, (((1,), (1,)), ((), ())), preferred_element_type=F32, **kw)


def _rms_kernel(x_ref, g_ref, o_ref):
    o_ref[...] = _rms(x_ref[...], g_ref[...]).astype(o_ref.dtype)


def rmsnorm(x, g, out_dtype):
    T = x.shape[0]
    D = g.shape[0]
    tm = min(512, T)
    return pl.pallas_call(
        _rms_kernel,
        grid=(T // tm,),
        in_specs=[pl.BlockSpec((tm, D), lambda i: (i, 0)),
                  pl.BlockSpec((1, D), lambda i: (0, 0))],
        out_specs=pl.BlockSpec((tm, D), lambda i: (i, 0)),
        out_shape=jax.ShapeDtypeStruct((T, D), out_dtype),
        compiler_params=_params("parallel"),
        name="rmsnorm",
    )(x, g.reshape(1, D))


DILATIONS = tuple(d for _, d in DIL_PAIRS)


def _rms_classes_kernel(x_ref, g_ref, o_ref, *rest):
    class_refs, scr = rest[:-1], rest[-1]
    y = _rms(x_ref[0], g_ref[...])
    o_ref[0] = y.astype(o_ref.dtype)
    tm, D = y.shape
    for c in range(D // LANES):
        cs = slice(c * LANES, (c + 1) * LANES)
        scr[c] = y[:, cs]
        for ref, d in zip(class_refs, [d for d in DILATIONS if d > 1]):
            for r in range(d):
                ref[0, r, :, cs] = scr[c, pl.ds(r, tm // d, stride=d), :].astype(ref.dtype)


def rmsnorm_with_classes(x3, g):
    B, S, _ = x3.shape
    D = g.shape[0]
    tm = min(512, S)
    ds = [d for d in DILATIONS if d > 1]
    return pl.pallas_call(
        _rms_classes_kernel,
        grid=(B, S // tm),
        in_specs=[pl.BlockSpec((1, tm, D), lambda b, i: (b, i, 0)),
                  pl.BlockSpec((1, D), lambda b, i: (0, 0))],
        out_specs=[pl.BlockSpec((1, tm, D), lambda b, i: (b, i, 0))]
        + [pl.BlockSpec((1, d, tm // d, D), lambda b, i: (b, 0, i, 0)) for d in ds],
        out_shape=[jax.ShapeDtypeStruct((B, S, D), BF16)]
        + [jax.ShapeDtypeStruct((B, d, S // d, D), BF16) for d in ds],
        scratch_shapes=[pltpu.VMEM((D // LANES, tm, LANES), F32)],
        compiler_params=_params("parallel", "parallel"),
        name="rmsnorm_classes",
    )(x3, g.reshape(1, D))


N_COL_TILES = IN_COLS // COL_TILE
A_ROPE_TILES = 2 * A_WIDTH // COL_TILE
B_FIRST_TILE = 3 * A_WIDTH // COL_TILE
B_TILES_PER_GROUP = 3 * B_WIDTH // COL_TILE
B_ROPE_PER_GROUP = 2 * B_WIDTH // COL_TILE
C_FIRST_TILE = B_FIRST_TILE + len(DIL_PAIRS) * B_TILES_PER_GROUP


def _inproj_kernel(x_ref, w_ref, ca_ref, sa_ref, cb_ref, sb_ref, o_ref, *, col_map):
    j = col_map(pl.program_id(1))
    acc = _dot(x_ref[...], w_ref[...])
    jb = j - B_FIRST_TILE
    is_a = j < A_ROPE_TILES
    is_b = (j >= B_FIRST_TILE) & (j < C_FIRST_TILE) & (jb % B_TILES_PER_GROUP < B_ROPE_PER_GROUP)

    @pl.when(is_a)
    def _():
        c, s = ca_ref[...], sa_ref[...]
        lane = lax.broadcasted_iota(jnp.int32, c.shape, 1)
        first = (lane % DA_QK_DIM) < (DA_QK_DIM // 2)
        for t in range(COL_TILE // LANES):
            x = acc[:, t * LANES:(t + 1) * LANES]
            rot = jnp.where(first, pltpu.roll(x, LANES - DA_QK_DIM // 2, 1),
                            pltpu.roll(x, DA_QK_DIM // 2, 1))
            o_ref[:, t * LANES:(t + 1) * LANES] = (x * c + rot * s).astype(o_ref.dtype)

    @pl.when(is_b)
    def _():
        c, s = cb_ref[...], sb_ref[...]
        for t in range(COL_TILE // LANES):
            x = acc[:, t * LANES:(t + 1) * LANES]
            rot = pltpu.roll(x, DIL_HEAD_DIM // 2, 1)
            o_ref[:, t * LANES:(t + 1) * LANES] = (x * c + rot * s).astype(o_ref.dtype)

    @pl.when(jnp.logical_not(is_a | is_b))
    def _():
        o_ref[...] = acc.astype(o_ref.dtype)


NAT_TILES = N_COL_TILES - (len(DIL_PAIRS) - 1) * B_TILES_PER_GROUP
NAT_COLS = NAT_TILES * COL_TILE
NAT_C_FIRST_TILE = B_FIRST_TILE + B_TILES_PER_GROUP


def _natural_col_map(j):
    return j + jnp.where(j >= NAT_C_FIRST_TILE, (len(DIL_PAIRS) - 1) * B_TILES_PER_GROUP, 0)


def in_projection(xn, w, ca, sa, cb, sb, group=None):
    T, D = xn.shape
    tm = min(1024, T)
    if group is None:
        n_tiles, col_map = NAT_TILES, _natural_col_map
    else:
        first = B_FIRST_TILE + group * B_TILES_PER_GROUP
        n_tiles, col_map = B_TILES_PER_GROUP, lambda j: j + first
    tab = pl.BlockSpec((tm, LANES), lambda i, j: (i, 0))
    return pl.pallas_call(
        functools.partial(_inproj_kernel, col_map=col_map),
        grid=(T // tm, n_tiles),
        in_specs=[pl.BlockSpec((tm, D), lambda i, j: (i, 0)),
                  pl.BlockSpec((D, COL_TILE), lambda i, j: (0, col_map(j))),
                  tab, tab, tab, tab],
        out_specs=pl.BlockSpec((tm, COL_TILE), lambda i, j: (i, j)),
        out_shape=jax.ShapeDtypeStruct((T, n_tiles * COL_TILE), BF16),
        compiler_params=_params("parallel", "arbitrary"),
        name="in_projection" if group is None else f"in_projection_g{group}",
    )(xn, w, ca, sa, cb, sb)


KV_UNROLL = 8


def _diff_attn_kernel(q_ref, k_ref, v_ref, lam_ref, li_ref, g_ref, o_ref, vext_ref, *, tk):
    S = k_ref.shape[1]
    tq = q_ref.shape[1]

    @pl.when(pl.program_id(2) == 0)
    def _():
        vext_ref[:, :DA_V_DIM] = v_ref[0]
        vext_ref[:, DA_V_DIM:] = jnp.ones((S, DA_V_DIM), BF16)

    q = q_ref[0] * jnp.asarray(DA_QK_DIM ** -0.5, BF16)
    lane = lax.broadcasted_iota(jnp.int32, q.shape, 1)
    zero = jnp.zeros_like(q)
    q0 = jnp.where(lane < DA_QK_DIM, q, zero)
    q1 = jnp.where(lane >= DA_QK_DIM, q, zero)

    def step(qm, kj, vj, m, a):
        s = _dot_nt(qm, kj)
        m_new = jnp.maximum(m, jnp.max(s, axis=-1, keepdims=True))
        p = jnp.exp((s - m_new).astype(BF16))
        a = jnp.exp(m - m_new) * a + _dot(p, vj)
        return m_new, a

    def body(j, carry):
        m0, a0, m1, a1 = carry
        off = pl.multiple_of(j * tk, tk)
        kj = k_ref[0, pl.ds(off, tk), :]
        vj = vext_ref[pl.ds(off, tk), :]
        m0, a0 = step(q0, kj, vj, m0, a0)
        m1, a1 = step(q1, kj, vj, m1, a1)
        return m0, a0, m1, a1

    mi = jnp.full((tq, 1), NEG_BIG, F32)
    ai = jnp.zeros((tq, 2 * DA_V_DIM), F32)
    _, a0, _, a1 = lax.fori_loop(0, S // tk, body, (mi, ai, mi, ai), unroll=KV_UNROLL)

    lp = lam_ref[...]
    lam_init = li_ref[...]
    lam = (jnp.exp(jnp.sum(lp[0:1] * lp[1:2], axis=-1, keepdims=True))
           - jnp.exp(jnp.sum(lp[2:3] * lp[3:4], axis=-1, keepdims=True)) + lam_init)
    o = a0[:, :DA_V_DIM] / a0[:, DA_V_DIM:] - lam * (a1[:, :DA_V_DIM] / a1[:, DA_V_DIM:])
    o_ref[0] = (_rms(o, g_ref[...]) * (1.0 - lam_init)).astype(o_ref.dtype)


def diff_attention(proj3, lam_p, lam_init, sub_gain):
    B, S, _ = proj3.shape
    tq = min(1024, S)
    tk = min(512, S)
    assert (S // tk) % KV_UNROLL == 0
    kb = A_WIDTH // DA_V_DIM
    return pl.pallas_call(
        functools.partial(_diff_attn_kernel, tk=tk),
        grid=(B, DA_HEADS, S // tq),
        in_specs=[pl.BlockSpec((1, tq, DA_V_DIM), lambda b, h, i: (b, i, h)),
                  pl.BlockSpec((1, S, DA_V_DIM), lambda b, h, i: (b, 0, kb + h)),
                  pl.BlockSpec((1, S, DA_V_DIM), lambda b, h, i: (b, 0, 2 * kb + h)),
                  pl.BlockSpec((4, DA_QK_DIM), lambda b, h, i: (0, 0)),
                  pl.BlockSpec((1, 1), lambda b, h, i: (0, 0)),
                  pl.BlockSpec((1, DA_V_DIM), lambda b, h, i: (0, 0))],
        out_specs=pl.BlockSpec((1, tq, DA_V_DIM), lambda b, h, i: (b, i, h)),
        out_shape=jax.ShapeDtypeStruct((B, S, A_WIDTH), BF16),
        scratch_shapes=[pltpu.VMEM((S, 2 * DA_V_DIM), BF16)],
        compiler_params=_params("parallel", "parallel", "arbitrary"),
        name="diff_attention",
    )(proj3, proj3, proj3, lam_p, lam_init, sub_gain.reshape(1, DA_V_DIM))


def _dilated_kernel(q_ref, kp_ref, kc_ref, kn_ref, vp_ref, vc_ref, vn_ref, o_ref, lse_ref, *,
                    n_side):
    i = pl.program_id(2)
    n_blk = pl.num_programs(2)
    rows = lax.broadcasted_iota(jnp.int32, (QBLK, 3 * QBLK), 0)
    cols = lax.broadcasted_iota(jnp.int32, (QBLK, 3 * QBLK), 1)
    rel = cols - QBLK - rows
    kpos = (i - 1) * QBLK + cols
    valid = (jnp.abs(rel) <= n_side) & (kpos >= 0) & (kpos < n_blk * QBLK)
    scale = DIL_HEAD_DIM ** -0.5
    for h in range(DIL_HEADS):
        hs = slice(h * DIL_HEAD_DIM, (h + 1) * DIL_HEAD_DIM)
        kcat = jnp.concatenate([kp_ref[0, 0, :, hs], kc_ref[0, 0, :, hs], kn_ref[0, 0, :, hs]], axis=0)
        vcat = jnp.concatenate([vp_ref[0, 0, :, hs], vc_ref[0, 0, :, hs], vn_ref[0, 0, :, hs]], axis=0)
        s = jnp.where(valid, _dot_nt(q_ref[0, 0, :, hs], kcat) * scale, NEG_BIG)
        m = jnp.max(s, axis=-1, keepdims=True)
        e = jnp.exp(s - m)
        l = jnp.sum(e, axis=-1, keepdims=True)
        o_ref[0, 0, :, hs] = _dot((e / l).astype(BF16), vcat)
        lse_ref[0, 0, :, hs] = jnp.broadcast_to(m + jnp.log(l), (QBLK, DIL_HEAD_DIM))


def dilated_attention(qkv, first_tile, window, dilation, group):
    B, d, L, _ = qkv.shape
    n_side = (window // 2) // dilation
    assert d == dilation and n_side <= QBLK and L % QBLK == 0
    nb = L // QBLK

    def spec(kind, shift):
        def imap(b, r, i):
            return (b, r, jnp.clip(i + shift, 0, nb - 1), first_tile + kind)
        return pl.BlockSpec((1, 1, QBLK, B_WIDTH), imap)

    out_spec = pl.BlockSpec((1, 1, QBLK, B_WIDTH), lambda b, r, i: (b, r, i, 0))
    return pl.pallas_call(
        functools.partial(_dilated_kernel, n_side=n_side),
        grid=(B, d, nb),
        in_specs=[spec(0, 0), spec(1, -1), spec(1, 0), spec(1, 1),
                  spec(2, -1), spec(2, 0), spec(2, 1)],
        out_specs=[out_spec, out_spec],
        out_shape=[jax.ShapeDtypeStruct((B, d, L, B_WIDTH), F32)] * 2,
        compiler_params=_params("parallel", "parallel", "arbitrary"),
        name=f"dilated_attention_g{group}",
    )(qkv, qkv, qkv, qkv, qkv, qkv, qkv)


def _merge_kernel(*refs):
    n = len(DILATIONS)
    ins, y_ref, scr = refs[:2 * n], refs[2 * n], refs[2 * n + 1:]
    tm = y_ref.shape[1]
    for c in range(B_WIDTH // LANES):
        cs = slice(c * LANES, (c + 1) * LANES)
        vals, k = [], 0
        for g, d in enumerate(DILATIONS):
            o_ref, l_ref = ins[2 * g], ins[2 * g + 1]
            if d == 1:
                vals.append((o_ref[0, 0, :, cs], l_ref[0, 0, :, cs]))
                continue
            so, sl = scr[2 * k], scr[2 * k + 1]
            k += 1
            for r in range(d):
                so[c, pl.ds(r, tm // d, stride=d), :] = o_ref[0, r, :, cs]
                sl[c, pl.ds(r, tm // d, stride=d), :] = l_ref[0, r, :, cs]
            vals.append((so[c], sl[c]))
        m = functools.reduce(jnp.maximum, [l for _, l in vals])
        ws = [jnp.exp(l - m) for _, l in vals]
        num = sum(w * o for w, (o, _) in zip(ws, vals))
        y_ref[0, :, cs] = (num / sum(ws)).astype(y_ref.dtype)


def merge_groups(outs, lses):
    B, _, S, _ = outs[0].shape
    tm = min(512, S)
    specs, args = [], []
    for d, o, l in zip(DILATIONS, outs, lses):
        sp = pl.BlockSpec((1, d, tm // d, B_WIDTH), lambda b, i: (b, 0, i, 0))
        specs += [sp, sp]
        args += [o, l]
    n_scr = 2 * sum(1 for d in DILATIONS if d > 1)
    return pl.pallas_call(
        _merge_kernel,
        grid=(B, S // tm),
        in_specs=specs,
        out_specs=pl.BlockSpec((1, tm, B_WIDTH), lambda b, i: (b, i, 0)),
        out_shape=jax.ShapeDtypeStruct((B, S, B_WIDTH), BF16),
        scratch_shapes=[pltpu.VMEM((B_WIDTH // LANES, tm, LANES), F32)] * n_scr,
        compiler_params=_params("parallel", "parallel"),
        name="merge_groups",
    )(*args)


HALO_ROWS = 16


def _conv_kernel(h_ref, gb_ref, gc_ref, hp_ref, gp_ref, hn_ref, gn_ref, w_ref, o_ref):
    i = pl.program_id(1)
    ts = h_ref.shape[1]
    u = gc_ref[0].astype(F32) * h_ref[0].astype(F32)
    up = (gp_ref[0].astype(F32) * hp_ref[0].astype(F32))[HALO_ROWS - 1:HALO_ROWS]
    un = (gn_ref[0].astype(F32) * hn_ref[0].astype(F32))[0:1]
    up = jnp.where(i > 0, up, 0.0)
    un = jnp.where(i < pl.num_programs(1) - 1, un, 0.0)
    rows = lax.broadcasted_iota(jnp.int32, u.shape, 0)
    u_prev = jnp.where(rows == 0, up, pltpu.roll(u, 1, 0))
    u_next = jnp.where(rows == ts - 1, un, pltpu.roll(u, ts - 1, 0))
    w = w_ref[...]
    y = gb_ref[0].astype(F32) * (w[0:1] * u_prev + w[1:2] * u + w[2:3] * u_next)
    o_ref[0] = y.astype(o_ref.dtype)


def short_conv_branch(proj3, conv_w):
    B, S, cols = proj3.shape
    ts = min(512, S)
    c0 = (cols - 3 * CONV_CH) // CONV_CH
    hb = ts // HALO_ROWS
    n_halo = S // HALO_ROWS

    def main(c):
        return pl.BlockSpec((1, ts, CONV_CH), lambda b, i: (b, i, c))

    def prev(c):
        return pl.BlockSpec((1, HALO_ROWS, CONV_CH), lambda b, i: (b, jnp.maximum(i * hb - 1, 0), c))

    def nxt(c):
        return pl.BlockSpec((1, HALO_ROWS, CONV_CH),
                            lambda b, i: (b, jnp.minimum((i + 1) * hb, n_halo - 1), c))

    return pl.pallas_call(
        _conv_kernel,
        grid=(B, S // ts),
        in_specs=[main(c0), main(c0 + 1), main(c0 + 2), prev(c0), prev(c0 + 2), nxt(c0), nxt(c0 + 2),
                  pl.BlockSpec((3, CONV_CH), lambda b, i: (0, 0))],
        out_specs=pl.BlockSpec((1, ts, CONV_CH), lambda b, i: (b, i, 0)),
        out_shape=jax.ShapeDtypeStruct((B, S, CONV_CH), BF16),
        compiler_params=_params("parallel", "parallel"),
        name="short_conv",
    )(proj3, proj3, proj3, proj3, proj3, proj3, proj3, conv_w)


def _gate_merge_kernel(xn_ref, ya_ref, yb_ref, yc_ref, wg0_ref, wg1_ref, wg2_ref,
                       b0_ref, b1_ref, b2_ref, wa_ref, wb_ref, wc_ref, o_ref):
    xn = xn_ref[...]
    acc = jax.nn.sigmoid(_dot(xn, wg0_ref[...]) + b0_ref[...]) * _dot(ya_ref[...], wa_ref[...])
    acc += jax.nn.sigmoid(_dot(xn, wg1_ref[...]) + b1_ref[...]) * _dot(yb_ref[...], wb_ref[...])
    acc += jax.nn.sigmoid(_dot(xn, wg2_ref[...]) + b2_ref[...]) * _dot(yc_ref[...], wc_ref[...])
    o_ref[...] = acc.astype(o_ref.dtype)


def gate_merge(xn, ya, yb, yc, w_gate, b_gate, w_br):
    T, D = xn.shape
    tm = min(512, T)
    tn = min(512, D)
    nj = D // tn
    assert A_WIDTH == B_WIDTH and C_WIDTH == 2 * A_WIDTH

    def wg(k):
        return pl.BlockSpec((D, tn), lambda i, j: (0, k * nj + j))

    def bg(k):
        return pl.BlockSpec((1, tn), lambda i, j: (0, k * nj + j))

    def rowblk(w):
        return pl.BlockSpec((tm, w), lambda i, j: (i, 0))

    return pl.pallas_call(
        _gate_merge_kernel,
        grid=(T // tm, nj),
        in_specs=[rowblk(D), rowblk(A_WIDTH), rowblk(B_WIDTH), rowblk(C_WIDTH),
                  wg(0), wg(1), wg(2), bg(0), bg(1), bg(2),
                  pl.BlockSpec((A_WIDTH, tn), lambda i, j: (0, j)),
                  pl.BlockSpec((B_WIDTH, tn), lambda i, j: (1, j)),
                  pl.BlockSpec((C_WIDTH, tn), lambda i, j: (1, j))],
        out_specs=pl.BlockSpec((tm, tn), lambda i, j: (i, j)),
        out_shape=jax.ShapeDtypeStruct((T, D), BF16),
        compiler_params=_params("parallel", "arbitrary"),
        name="gate_merge",
    )(xn, ya, yb, yc, w_gate, w_gate, w_gate, b_gate, b_gate, b_gate, w_br, w_br, w_br)


def _matmul_residual_kernel(x_ref, w_ref, r_ref, o_ref):
    o_ref[...] = r_ref[...] + _dot(x_ref[...], w_ref[...])


def matmul_residual(x, w, res):
    T, K = x.shape
    N = w.shape[1]
    tm = min(1024, T)
    tn = min(1024, N)
    return pl.pallas_call(
        _matmul_residual_kernel,
        grid=(T // tm, N // tn),
        in_specs=[pl.BlockSpec((tm, K), lambda i, j: (i, 0)),
                  pl.BlockSpec((K, tn), lambda i, j: (0, j)),
                  pl.BlockSpec((tm, tn), lambda i, j: (i, j))],
        out_specs=pl.BlockSpec((tm, tn), lambda i, j: (i, j)),
        out_shape=jax.ShapeDtypeStruct((T, N), F32),
        compiler_params=_params("parallel", "arbitrary"),
        name="matmul_residual",
    )(x, w, res)


def _memkv_kernel(mem_ref, g_ref, w_ref, o_ref):
    o_ref[0, 0] = _dot(_rms(mem_ref[0], g_ref[0]).astype(BF16), w_ref[0]).astype(o_ref.dtype)


def memory_kv(mem, norm_mem, wkv):
    B, M, D = mem.shape
    L, _, N = wkv.shape
    return pl.pallas_call(
        _memkv_kernel,
        grid=(L, B),
        in_specs=[pl.BlockSpec((1, M, D), lambda l, b: (b, 0, 0)),
                  pl.BlockSpec((1, 1, D), lambda l, b: (l, 0, 0)),
                  pl.BlockSpec((1, D, N), lambda l, b: (l, 0, 0))],
        out_specs=pl.BlockSpec((1, 1, M, N), lambda l, b: (l, b, 0, 0)),
        out_shape=jax.ShapeDtypeStruct((L, B, M, N), BF16),
        compiler_params=_params("parallel", "parallel"),
        name="memory_kv",
    )(mem, norm_mem.reshape(L, 1, D), wkv)


def _cross_kernel(h_ref, gx_ref, wq_ref, kv_ref, wo_ref, gm_ref, wr_ref, hx_ref, aff_ref):
    h = h_ref[0]
    D = h.shape[1]
    q = _dot(_rms(h, gx_ref[...]).astype(BF16), wq_ref[...]) * (X_HEAD_DIM ** -0.5)
    q = q.astype(BF16)
    kv = kv_ref[0]
    outs = []
    for hd in range(X_HEADS):
        ks = slice(hd * X_HEAD_DIM, (hd + 1) * X_HEAD_DIM)
        vs = slice((X_HEADS + hd) * X_HEAD_DIM, (X_HEADS + hd + 1) * X_HEAD_DIM)
        s = _dot_nt(q[:, ks], kv[:, ks])
        e = jnp.exp(s - jnp.max(s, axis=-1, keepdims=True))
        p = e / jnp.sum(e, axis=-1, keepdims=True)
        outs.append(_dot(p.astype(BF16), kv[:, vs]))
    o = jnp.concatenate(outs, axis=1).astype(BF16)
    hn = h + _dot(o, wo_ref[...])
    xn = _rms(hn, gm_ref[...])
    hx_ref[0, :, :D] = hn
    hx_ref[0, :, D:] = xn
    lt = _dot_nt(wr_ref[...], xn, precision=lax.Precision.HIGHEST)
    e = jnp.exp(lt - jnp.max(lt, axis=0, keepdims=True))
    aff_ref[0] = e / jnp.sum(e, axis=0, keepdims=True)


def cross_attention_router(h3, g_cross, wq, kv, wo, g_moe, w_router_t):
    B, S, D = h3.shape
    M, NKV = kv.shape[1:]
    E = w_router_t.shape[0]
    NQ = wq.shape[1]
    tm = min(512, S)

    def full(shape):
        return pl.BlockSpec(shape, lambda b, i: (0,) * len(shape))

    return pl.pallas_call(
        _cross_kernel,
        grid=(B, S // tm),
        in_specs=[pl.BlockSpec((1, tm, D), lambda b, i: (b, i, 0)),
                  full((1, D)), full((D, NQ)),
                  pl.BlockSpec((1, M, NKV), lambda b, i: (b, 0, 0)),
                  full((NQ, D)), full((1, D)), full((E, D))],
        out_specs=[pl.BlockSpec((1, tm, 2 * D), lambda b, i: (b, i, 0)),
                   pl.BlockSpec((1, E, tm), lambda b, i: (b, 0, i))],
        out_shape=[jax.ShapeDtypeStruct((B, S, 2 * D), F32),
                   jax.ShapeDtypeStruct((B, E, S), F32)],
        compiler_params=_params("parallel", "parallel"),
        name="cross_attention_router",
    )(h3, g_cross.reshape(1, D), wq, kv, wo, g_moe.reshape(1, D), w_router_t)


SLOT_BLK = 64


def _select_kernel(aff_ref, idx_ref, gate_ref, cs_ref, w_ref, ts_ref, *, cap):
    E, C, _ = aff_ref.shape[1:]
    R = E * C
    a = aff_ref[0].reshape(R, LANES)
    bits = pltpu.bitcast(a, jnp.int32)
    rid = lax.broadcasted_iota(jnp.int32, (R, LANES), 0)

    ii = lax.broadcasted_iota(jnp.int32, (LANES, LANES), 0)
    jj = lax.broadcasted_iota(jnp.int32, (LANES, LANES), 1)
    upper = jnp.where(ii <= jj, 1.0, 0.0).astype(BF16)
    ones = jnp.ones((LANES, LANES), BF16)
    ri = lax.broadcasted_iota(jnp.int32, (R, R), 0) // C
    rj = lax.broadcasted_iota(jnp.int32, (R, R), 1)
    same_expert = jnp.where(rj // C == ri, 1.0, 0.0).astype(BF16)
    rows_before = jnp.where((rj // C == ri) & (rj < lax.broadcasted_iota(jnp.int32, (R, R), 0)),
                            1.0, 0.0).astype(BF16)

    def row_totals(x):
        return _dot(x.astype(BF16), ones).astype(BF16)

    def count(x):
        return _dot(same_expert, row_totals(x))

    def prefix_count(x):
        return _dot(x.astype(BF16), upper) + _dot(rows_before, row_totals(x))

    thr = jnp.zeros((R, LANES), jnp.int32)
    for bit in range(30, -1, -1):
        cand = thr | jnp.int32(1 << bit)
        thr = jnp.where(count(jnp.where(bits >= cand, 1.0, 0.0)) >= cap, cand, thr)
    gt_f = jnp.where(bits > thr, 1.0, 0.0)
    eq_f = jnp.where(bits == thr, 1.0, 0.0)

    need = cap - count(gt_f)
    eq_rank = prefix_count(eq_f) - eq_f
    sel = gt_f + eq_f * jnp.where(eq_rank < need, 1.0, 0.0)
    csum = prefix_count(sel)
    tok = ((rid % C) * LANES + lax.broadcasted_iota(jnp.int32, (R, LANES), 1)).astype(F32)
    cs_ref[...] = csum
    w_ref[...] = a * sel
    ts_ref[...] = tok * sel

    def per_expert(e, _):
        for sb in range(cap // SLOT_BLK):
            k1 = (lax.broadcasted_iota(jnp.int32, (SLOT_BLK, LANES), 0) + (sb * SLOT_BLK + 1)).astype(F32)

            def sweep(c, acc):
                ai, ag = acc
                r = e * C + c
                hit = cs_ref[pl.ds(r, 1), :] == k1
                ai = ai + jnp.where(hit, ts_ref[pl.ds(r, 1), :], 0.0)
                ag = ag + jnp.where(hit, w_ref[pl.ds(r, 1), :], 0.0)
                return ai, ag

            z = jnp.zeros((SLOT_BLK, LANES), F32)
            ai, ag = lax.fori_loop(0, C, sweep, (z, z))
            rows = pl.ds(sb * SLOT_BLK, SLOT_BLK)
            idx_ref[0, e, rows, :] = jnp.sum(ai, axis=1, keepdims=True).astype(jnp.int32)
            gate_ref[0, e, rows, :] = jnp.sum(ag, axis=1, keepdims=True)
        return 0

    lax.fori_loop(0, E, per_expert, 0)


def expert_choice_select(aff_t, cap):
    B, E, S = aff_t.shape
    C = S // LANES
    assert cap % SLOT_BLK == 0
    idx, gate = pl.pallas_call(
        functools.partial(_select_kernel, cap=cap),
        grid=(B,),
        in_specs=[pl.BlockSpec((1, E, C, LANES), lambda b: (b, 0, 0, 0))],
        out_specs=[pl.BlockSpec((1, E, cap, 1), lambda b: (b, 0, 0, 0)),
                   pl.BlockSpec((1, E, cap, 1), lambda b: (b, 0, 0, 0))],
        out_shape=[jax.ShapeDtypeStruct((B, E, cap, 1), jnp.int32),
                   jax.ShapeDtypeStruct((B, E, cap, 1), F32)],
        scratch_shapes=[pltpu.VMEM((E * C, LANES), F32)] * 3,
        compiler_params=_params("parallel"),
        name="expert_choice_select",
    )(aff_t.reshape(B, E, C, LANES))
    return idx, gate


def _expert_ffn_kernel(idx_hbm, gate_ref, hx_in_hbm, w1_ref, w3_ref, w2_ref, hx_hbm,
                       idx_smem, buf, sems, *, sub, seq_len):
    del hx_in_hbm
    e = pl.program_id(0)
    n_exp = pl.num_programs(0)
    n_seq = gate_ref.shape[0]
    cap = idx_smem.shape[0] // n_seq
    D = w2_ref.shape[2]
    per_seq = cap // sub
    n_tiles = n_seq * per_seq
    for b in range(n_seq):
        cp = pltpu.make_async_copy(idx_hbm.at[pl.ds((b * n_exp + e) * cap, cap)],
                                   idx_smem.at[pl.ds(b * cap, cap)], sems.at[0])
        cp.start()
        cp.wait()

    def token_row(t, r):
        return pl.ds((t // per_seq) * seq_len + idx_smem[t * sub + r], 1)

    def gather(t, slot):
        def row(r, _):
            pltpu.make_async_copy(hx_hbm.at[token_row(t, r), :], buf.at[slot, pl.ds(r, 1), :],
                                  sems.at[1 + slot]).start()
            return 0
        lax.fori_loop(0, sub, row, 0, unroll=8)

    def wait_gather(slot):
        pltpu.make_async_copy(hx_hbm.at[pl.ds(0, sub), :], buf.at[slot], sems.at[1 + slot]).wait()

    def write_back(t, slot):
        def row(r, _):
            pltpu.make_async_copy(buf.at[slot, pl.ds(r, 1), pl.ds(0, D)],
                                  hx_hbm.at[token_row(t, r), pl.ds(0, D)], sems.at[3 + slot]).start()
            return 0
        lax.fori_loop(0, sub, row, 0, unroll=8)

    def wait_write_back(slot):
        pltpu.make_async_copy(buf.at[slot, :, pl.ds(0, D)], hx_hbm.at[pl.ds(0, sub), pl.ds(0, D)],
                              sems.at[3 + slot]).wait()

    def tile(t, slot):
        @pl.when(t + 1 < n_tiles)
        def _():
            @pl.when(t >= 1)
            def _():
                wait_write_back(1 - slot)
            gather(t + 1, 1 - slot)

        wait_gather(slot)
        xg = buf[slot, :, D:].astype(BF16)
        a = _dot(xg, w1_ref[0])
        c = _dot(xg, w3_ref[0])
        hid = (a * jax.nn.sigmoid(a) * c).astype(BF16)
        y = _dot(hid, w2_ref[0])
        off = pl.multiple_of((t % per_seq) * sub, sub)
        g = gate_ref[t // per_seq, 0, pl.ds(off, sub), :]
        buf[slot, :, :D] = buf[slot, :, :D] + y * g
        write_back(t, slot)

    gather(0, 0)

    def pair(p, _):
        tile(2 * p, 0)
        tile(2 * p + 1, 1)
        return 0

    lax.fori_loop(0, n_tiles // 2, pair, 0)
    wait_write_back(0)
    wait_write_back(1)


def expert_ffn(idx, gate, hx, w1, w3, w2, seq_len):
    T, D2 = hx.shape
    B, E, cap, _ = gate.shape
    D, F = w1.shape[1:]
    sub = min(256, cap)
    assert D2 == 2 * D and (B * cap // sub) % 2 == 0 and cap % 1024 == 0
    return pl.pallas_call(
        functools.partial(_expert_ffn_kernel, sub=sub, seq_len=seq_len),
        grid=(E,),
        in_specs=[pl.BlockSpec(memory_space=pl.ANY),
                  pl.BlockSpec((B, 1, cap, 1), lambda e: (0, e, 0, 0)),
                  pl.BlockSpec(memory_space=pl.ANY),
                  pl.BlockSpec((1, D, F), lambda e: (e, 0, 0)),
                  pl.BlockSpec((1, D, F), lambda e: (e, 0, 0)),
                  pl.BlockSpec((1, F, D), lambda e: (e, 0, 0))],
        out_specs=pl.BlockSpec(memory_space=pl.ANY),
        out_shape=jax.ShapeDtypeStruct((T, D2), F32),
        scratch_shapes=[pltpu.SMEM((B * cap,), jnp.int32),
                        pltpu.VMEM((2, sub, D2), F32),
                        pltpu.SemaphoreType.DMA((5,))],
        input_output_aliases={2: 0},
        compiler_params=_params("arbitrary"),
        name="expert_ffn",
    )(idx.reshape(B * E * cap), gate, hx, w1, w3, w2)


def _rope_tables(positions, dim, reps):
    inv = ROPE_THETA ** (-jnp.arange(0, dim, 2, dtype=F32) / dim)
    ang = positions.astype(F32).reshape(-1, 1) * inv
    cos, sin = jnp.cos(ang), jnp.sin(ang)
    return (jnp.tile(jnp.concatenate([cos, cos], axis=-1), (1, reps)),
            jnp.tile(jnp.concatenate([-sin, sin], axis=-1), (1, reps)))


def _by_class(a, d):
    B, S = a.shape[:2]
    return jnp.swapaxes(a.reshape(B, S // d, d, *a.shape[2:]), 1, 2)


def kernel(x, mem, positions, norm_mix, w_in, da_lambda, da_subln, conv_w, w_branch, w_gate, b_gate,
           w_out, norm_cross, norm_mem, wq_x, wkv_x, wo_x, norm_moe, w_router, w_e1, w_e3, w_e2,
           norm_final):
    B, S, D = x.shape
    depth = w_in.shape[0]
    E = w_router.shape[-1]
    cap = CAPACITY_FACTOR * S // E
    T = B * S

    ca, sa = _rope_tables(positions, DA_QK_DIM, LANES // DA_QK_DIM)
    tabs_b = [_rope_tables(_by_class(positions, d), DIL_HEAD_DIM, 1) for d in DILATIONS]
    kv_all = memory_kv(mem, norm_mem, wkv_x.astype(BF16))

    hx = x
    for l in range(depth):
        lam_init = jnp.full((1, 1), 0.8 - 0.6 * math.exp(-0.3 * l), F32)
        w_in_l = w_in[l].astype(BF16)
        xn, *xn_cls = rmsnorm_with_classes(hx, norm_mix[l])
        xn = xn.reshape(T, D)
        proj3 = in_projection(xn, w_in_l, ca, sa, *tabs_b[0]).reshape(B, S, NAT_COLS)
        ya = diff_attention(proj3, da_lambda[l], lam_init, da_subln[l]).reshape(T, A_WIDTH)
        outs, lses = [], []
        for g, (window, d) in enumerate(DIL_PAIRS):
            if d == 1:
                qkv, first = proj3.reshape(B, 1, S, NAT_COLS), B_FIRST_TILE
            else:
                xg = xn_cls[[dd for dd in DILATIONS if dd > 1].index(d)].reshape(T, D)
                qkv = in_projection(xg, w_in_l, ca, sa, *tabs_b[g], group=g)
                qkv, first = qkv.reshape(B, d, S // d, B_TILES_PER_GROUP * COL_TILE), 0
            o, lse = dilated_attention(qkv, first, window, d, g)
            outs.append(o)
            lses.append(lse)
        yb = merge_groups(outs, lses).reshape(T, B_WIDTH)
        yc = short_conv_branch(proj3, conv_w[l]).reshape(T, C_WIDTH)
        merged = gate_merge(xn, ya, yb, yc, w_gate[l].astype(BF16), b_gate[l].reshape(1, -1),
                            w_branch[l].astype(BF16))
        h = matmul_residual(merged, w_out[l].astype(BF16), hx.reshape(T, -1))
        hx, aff_t = cross_attention_router(
            h.reshape(B, S, D), norm_cross[l], wq_x[l].astype(BF16), kv_all[l],
            wo_x[l].astype(BF16), norm_moe[l], w_router[l].T)
        idx, gate = expert_choice_select(aff_t, cap)
        hx = expert_ffn(idx, gate, hx.reshape(T, 2 * D), w_e1[l].astype(BF16),
                        w_e3[l].astype(BF16), w_e2[l].astype(BF16), S).reshape(B, S, 2 * D)
    return rmsnorm(hx.reshape(T, -1), norm_final, F32).reshape(B, S, D)
```

```python
import functools
import math

import jax
import jax.numpy as jnp
from jax import lax
from jax.experimental import pallas as pl
from jax.experimental.pallas import tpu as pltpu

F32 = jnp.float32
BF16 = jnp.bfloat16

EPS = 1e-6
ROPE_THETA = 10000.0
DA_HEADS = 4
DA_QK_DIM = 64
DA_V_DIM = 2 * DA_QK_DIM
DIL_PAIRS = ((128, 1), (512, 4), (2048, 16))
DIL_HEADS = 4
DIL_HEAD_DIM = 128
CONV_CH = 1024
A_WIDTH = DA_HEADS * DA_V_DIM
B_WIDTH = DIL_HEADS * DIL_HEAD_DIM
C_WIDTH = CONV_CH
MIX_WIDTH = A_WIDTH + B_WIDTH + C_WIDTH
N_BRANCH = 3
IN_COLS = 3 * A_WIDTH + len(DIL_PAIRS) * 3 * B_WIDTH + 3 * CONV_CH
X_HEADS = 4
X_HEAD_DIM = 128
CAPACITY_FACTOR = 2

LANES = 128
V7X_VMEM_BYTES = 64 * 1024 * 1024
VMEM_LIMIT = V7X_VMEM_BYTES - 8 * 1024 * 1024
COL_TILE = 512
BF16_ROWS = 16
DIL_QBLK = 256
NEG_BIG = -1e30


def _params(*sem):
    return pltpu.CompilerParams(dimension_semantics=sem, vmem_limit_bytes=VMEM_LIMIT)


def _rms(x, g):
    return x * lax.rsqrt(jnp.mean(x * x, axis=-1, keepdims=True) + EPS) * g


def _dot(a, b):
    return jnp.dot(a, b, preferred_element_type=F32)


def _dot_nt(a, b, **kw):
    return lax.dot_general(a, b, (((1,), (1,)), ((), ())), preferred_element_type=F32, **kw)


def _rms_kernel(x_ref, g_ref, o_ref):
    o_ref[...] = _rms(x_ref[...], g_ref[...]).astype(o_ref.dtype)


def rmsnorm(x, g, out_dtype):
    T = x.shape[0]
    D = g.shape[0]
    tm = min(512, T)
    return pl.pallas_call(
        _rms_kernel,
        grid=(T // tm,),
        in_specs=[pl.BlockSpec((tm, D), lambda i: (i, 0)),
                  pl.BlockSpec((1, D), lambda i: (0, 0))],
        out_specs=pl.BlockSpec((tm, D), lambda i: (i, 0)),
        out_shape=jax.ShapeDtypeStruct((T, D), out_dtype),
        compiler_params=_params("parallel"),
        name="rmsnorm",
    )(x, g.reshape(1, D))


DILATIONS = tuple(d for _, d in DIL_PAIRS)


def _rms_classes_kernel(x_ref, g_ref, o_ref, *rest):
    class_refs, scr = rest[:-1], rest[-1]
    y = _rms(x_ref[0], g_ref[...])
    o_ref[0] = y.astype(o_ref.dtype)
    tm, D = y.shape
    for c in range(D // LANES):
        cs = slice(c * LANES, (c + 1) * LANES)
        scr[c] = y[:, cs]
        for ref, d in zip(class_refs, [d for d in DILATIONS if d > 1]):
            for r in range(d):
                ref[0, r, :, cs] = scr[c, pl.ds(r, tm // d, stride=d), :].astype(ref.dtype)


def rmsnorm_with_classes(x3, g):
    B, S, _ = x3.shape
    D = g.shape[0]
    tm = min(512, S)
    ds = [d for d in DILATIONS if d > 1]
    return pl.pallas_call(
        _rms_classes_kernel,
        grid=(B, S // tm),
        in_specs=[pl.BlockSpec((1, tm, D), lambda b, i: (b, i, 0)),
                  pl.BlockSpec((1, D), lambda b, i: (0, 0))],
        out_specs=[pl.BlockSpec((1, tm, D), lambda b, i: (b, i, 0))]
        + [pl.BlockSpec((1, d, tm // d, D), lambda b, i: (b, 0, i, 0)) for d in ds],
        out_shape=[jax.ShapeDtypeStruct((B, S, D), BF16)]
        + [jax.ShapeDtypeStruct((B, d, S // d, D), BF16) for d in ds],
        scratch_shapes=[pltpu.VMEM((D // LANES, tm, LANES), F32)],
        compiler_params=_params("parallel", "parallel"),
        name="rmsnorm_classes",
    )(x3, g.reshape(1, D))


N_COL_TILES = IN_COLS // COL_TILE
A_ROPE_TILES = 2 * A_WIDTH // COL_TILE
B_FIRST_TILE = 3 * A_WIDTH // COL_TILE
B_TILES_PER_GROUP = 3 * B_WIDTH // COL_TILE
B_ROPE_PER_GROUP = 2 * B_WIDTH // COL_TILE
C_FIRST_TILE = B_FIRST_TILE + len(DIL_PAIRS) * B_TILES_PER_GROUP


def _inproj_kernel(x_ref, w_ref, ca_ref, sa_ref, cb_ref, sb_ref, o_ref, *, col_map):
    j = col_map(pl.program_id(1))
    acc = _dot(x_ref[...], w_ref[...])
    jb = j - B_FIRST_TILE
    is_a = j < A_ROPE_TILES
    is_b = (j >= B_FIRST_TILE) & (j < C_FIRST_TILE) & (jb % B_TILES_PER_GROUP < B_ROPE_PER_GROUP)

    @pl.when(is_a)
    def _():
        c, s = ca_ref[...], sa_ref[...]
        lane = lax.broadcasted_iota(jnp.int32, c.shape, 1)
        first = (lane % DA_QK_DIM) < (DA_QK_DIM // 2)
        for t in range(COL_TILE // LANES):
            x = acc[:, t * LANES:(t + 1) * LANES]
            rot = jnp.where(first, pltpu.roll(x, LANES - DA_QK_DIM // 2, 1),
                            pltpu.roll(x, DA_QK_DIM // 2, 1))
            o_ref[:, t * LANES:(t + 1) * LANES] = (x * c + rot * s).astype(o_ref.dtype)

    @pl.when(is_b)
    def _():
        c, s = cb_ref[...], sb_ref[...]
        for t in range(COL_TILE // LANES):
            x = acc[:, t * LANES:(t + 1) * LANES]
            rot = pltpu.roll(x, DIL_HEAD_DIM // 2, 1)
            o_ref[:, t * LANES:(t + 1) * LANES] = (x * c + rot * s).astype(o_ref.dtype)

    @pl.when(jnp.logical_not(is_a | is_b))
    def _():
        o_ref[...] = acc.astype(o_ref.dtype)


NAT_TILES = N_COL_TILES - (len(DIL_PAIRS) - 1) * B_TILES_PER_GROUP
NAT_COLS = NAT_TILES * COL_TILE
NAT_C_FIRST_TILE = B_FIRST_TILE + B_TILES_PER_GROUP


def _natural_col_map(j):
    return j + jnp.where(j >= NAT_C_FIRST_TILE, (len(DIL_PAIRS) - 1) * B_TILES_PER_GROUP, 0)


def in_projection(xn, w, ca, sa, cb, sb, group=None):
    T, D = xn.shape
    tm = min(1024, T)
    if group is None:
        n_tiles, col_map = NAT_TILES, _natural_col_map
    else:
        first = B_FIRST_TILE + group * B_TILES_PER_GROUP
        n_tiles, col_map = B_TILES_PER_GROUP, lambda j: j + first
    tab = pl.BlockSpec((tm, LANES), lambda i, j: (i, 0))
    return pl.pallas_call(
        functools.partial(_inproj_kernel, col_map=col_map),
        grid=(T // tm, n_tiles),
        in_specs=[pl.BlockSpec((tm, D), lambda i, j: (i, 0)),
                  pl.BlockSpec((D, COL_TILE), lambda i, j: (0, col_map(j))),
                  tab, tab, tab, tab],
        out_specs=pl.BlockSpec((tm, COL_TILE), lambda i, j: (i, j)),
        out_shape=jax.ShapeDtypeStruct((T, n_tiles * COL_TILE), BF16),
        compiler_params=_params("parallel", "arbitrary"),
        name="in_projection" if group is None else f"in_projection_g{group}",
    )(xn, w, ca, sa, cb, sb)


KV_UNROLL = 8


def _diff_attn_kernel(q_ref, k_ref, v_ref, lam_ref, li_ref, g_ref, o_ref, vext_ref, *, tk):
    S = k_ref.shape[1]
    tq = q_ref.shape[1]

    @pl.when(pl.program_id(2) == 0)
    def _():
        vext_ref[:, :DA_V_DIM] = v_ref[0]
        vext_ref[:, DA_V_DIM:] = jnp.ones((S, DA_V_DIM), BF16)

    q = q_ref[0] * jnp.asarray(DA_QK_DIM ** -0.5, BF16)
    lane = lax.broadcasted_iota(jnp.int32, q.shape, 1)
    zero = jnp.zeros_like(q)
    q0 = jnp.where(lane < DA_QK_DIM, q, zero)
    q1 = jnp.where(lane >= DA_QK_DIM, q, zero)

    def step(qm, kj, vj, m, a):
        s = _dot_nt(qm, kj)
        m_new = jnp.maximum(m, jnp.max(s, axis=-1, keepdims=True))
        p = jnp.exp((s - m_new).astype(BF16))
        a = jnp.exp(m - m_new) * a + _dot(p, vj)
        return m_new, a

    def body(j, carry):
        m0, a0, m1, a1 = carry
        off = pl.multiple_of(j * tk, tk)
        kj = k_ref[0, pl.ds(off, tk), :]
        vj = vext_ref[pl.ds(off, tk), :]
        m0, a0 = step(q0, kj, vj, m0, a0)
        m1, a1 = step(q1, kj, vj, m1, a1)
        return m0, a0, m1, a1

    mi = jnp.full((tq, 1), NEG_BIG, F32)
    ai = jnp.zeros((tq, 2 * DA_V_DIM), F32)
    _, a0, _, a1 = lax.fori_loop(0, S // tk, body, (mi, ai, mi, ai), unroll=KV_UNROLL)

    lp = lam_ref[...]
    lam_init = li_ref[...]
    lam = (jnp.exp(jnp.sum(lp[0:1] * lp[1:2], axis=-1, keepdims=True))
           - jnp.exp(jnp.sum(lp[2:3] * lp[3:4], axis=-1, keepdims=True)) + lam_init)
    o = a0[:, :DA_V_DIM] / a0[:, DA_V_DIM:] - lam * (a1[:, :DA_V_DIM] / a1[:, DA_V_DIM:])
    o_ref[0] = (_rms(o, g_ref[...]) * (1.0 - lam_init)).astype(o_ref.dtype)


def diff_attention(proj3, lam_p, lam_init, sub_gain):
    B, S, _ = proj3.shape
    tq = min(1024, S)
    tk = min(512, S)
    assert (S // tk) % KV_UNROLL == 0
    kb = A_WIDTH // DA_V_DIM
    return pl.pallas_call(
        functools.partial(_diff_attn_kernel, tk=tk),
        grid=(B, DA_HEADS, S // tq),
        in_specs=[pl.BlockSpec((1, tq, DA_V_DIM), lambda b, h, i: (b, i, h)),
                  pl.BlockSpec((1, S, DA_V_DIM), lambda b, h, i: (b, 0, kb + h)),
                  pl.BlockSpec((1, S, DA_V_DIM), lambda b, h, i: (b, 0, 2 * kb + h)),
                  pl.BlockSpec((4, DA_QK_DIM), lambda b, h, i: (0, 0)),
                  pl.BlockSpec((1, 1), lambda b, h, i: (0, 0)),
                  pl.BlockSpec((1, DA_V_DIM), lambda b, h, i: (0, 0))],
        out_specs=pl.BlockSpec((1, tq, DA_V_DIM), lambda b, h, i: (b, i, h)),
        out_shape=jax.ShapeDtypeStruct((B, S, A_WIDTH), BF16),
        scratch_shapes=[pltpu.VMEM((S, 2 * DA_V_DIM), BF16)],
        compiler_params=_params("parallel", "parallel", "arbitrary"),
        name="diff_attention",
    )(proj3, proj3, proj3, lam_p, lam_init, sub_gain.reshape(1, DA_V_DIM))


def _dilated_kernel(q_ref, k_ref, v_ref, o_ref, lse_ref, *, n_side):
    i = pl.program_id(2)
    tq = q_ref.shape[2]
    L = k_ref.shape[2]
    W = tq + 2 * n_side
    start = pl.multiple_of(jnp.clip(i * tq - n_side, 0, L - W), n_side)
    qpos = i * tq + lax.broadcasted_iota(jnp.int32, (tq, W), 0)
    kpos = start + lax.broadcasted_iota(jnp.int32, (tq, W), 1)
    valid = jnp.abs(kpos - qpos) <= n_side
    scale = DIL_HEAD_DIM ** -0.5
    for h in range(DIL_HEADS):
        hs = slice(h * DIL_HEAD_DIM, (h + 1) * DIL_HEAD_DIM)
        kw = k_ref[0, 0, pl.ds(start, W), hs]
        vw = v_ref[0, 0, pl.ds(start, W), hs]
        s = jnp.where(valid, _dot_nt(q_ref[0, 0, :, hs], kw) * scale, NEG_BIG)
        m = jnp.max(s, axis=-1, keepdims=True)
        e = jnp.exp(s - m)
        l = jnp.sum(e, axis=-1, keepdims=True)
        o_ref[0, 0, :, hs] = _dot((e / l).astype(BF16), vw)
        lse_ref[0, 0, :, hs] = jnp.broadcast_to(m + jnp.log(l), (tq, DIL_HEAD_DIM))


def dilated_attention(qkv, first_tile, window, dilation, group):
    B, d, L, _ = qkv.shape
    n_side = (window // 2) // dilation
    tq = next(t for t in (DIL_QBLK, DIL_QBLK // 2) if t + 2 * n_side <= L and L % t == 0)
    assert d == dilation and n_side % BF16_ROWS == 0

    def resident(kind):
        return pl.BlockSpec((1, 1, L, B_WIDTH), lambda b, r, i: (b, r, 0, first_tile + kind))

    out_spec = pl.BlockSpec((1, 1, tq, B_WIDTH), lambda b, r, i: (b, r, i, 0))
    return pl.pallas_call(
        functools.partial(_dilated_kernel, n_side=n_side),
        grid=(B, d, L // tq),
        in_specs=[pl.BlockSpec((1, 1, tq, B_WIDTH), lambda b, r, i: (b, r, i, first_tile)),
                  resident(1), resident(2)],
        out_specs=[out_spec, out_spec],
        out_shape=[jax.ShapeDtypeStruct((B, d, L, B_WIDTH), F32)] * 2,
        compiler_params=_params("parallel", "parallel", "arbitrary"),
        name=f"dilated_attention_g{group}",
    )(qkv, qkv, qkv)


def _merge_kernel(*refs):
    n = len(DILATIONS)
    ins, y_ref, scr = refs[:2 * n], refs[2 * n], refs[2 * n + 1:]
    tm = y_ref.shape[1]
    for c in range(B_WIDTH // LANES):
        cs = slice(c * LANES, (c + 1) * LANES)
        vals, k = [], 0
        for g, d in enumerate(DILATIONS):
            o_ref, l_ref = ins[2 * g], ins[2 * g + 1]
            if d == 1:
                vals.append((o_ref[0, 0, :, cs], l_ref[0, 0, :, cs]))
                continue
            so, sl = scr[2 * k], scr[2 * k + 1]
            k += 1
            for r in range(d):
                so[c, pl.ds(r, tm // d, stride=d), :] = o_ref[0, r, :, cs]
                sl[c, pl.ds(r, tm // d, stride=d), :] = l_ref[0, r, :, cs]
            vals.append((so[c], sl[c]))
        m = functools.reduce(jnp.maximum, [l for _, l in vals])
        ws = [jnp.exp(l - m) for _, l in vals]
        num = sum(w * o for w, (o, _) in zip(ws, vals))
        y_ref[0, :, cs] = (num / sum(ws)).astype(y_ref.dtype)


def merge_groups(outs, lses):
    B, _, S, _ = outs[0].shape
    tm = min(512, S)
    specs, args = [], []
    for d, o, l in zip(DILATIONS, outs, lses):
        sp = pl.BlockSpec((1, d, tm // d, B_WIDTH), lambda b, i: (b, 0, i, 0))
        specs += [sp, sp]
        args += [o, l]
    n_scr = 2 * sum(1 for d in DILATIONS if d > 1)
    return pl.pallas_call(
        _merge_kernel,
        grid=(B, S // tm),
        in_specs=specs,
        out_specs=pl.BlockSpec((1, tm, B_WIDTH), lambda b, i: (b, i, 0)),
        out_shape=jax.ShapeDtypeStruct((B, S, B_WIDTH), BF16),
        scratch_shapes=[pltpu.VMEM((B_WIDTH // LANES, tm, LANES), F32)] * n_scr,
        compiler_params=_params("parallel", "parallel"),
        name="merge_groups",
    )(*args)


HALO_ROWS = 16


def _conv_kernel(h_ref, gb_ref, gc_ref, hp_ref, gp_ref, hn_ref, gn_ref, w_ref, o_ref):
    i = pl.program_id(1)
    ts = h_ref.shape[1]
    u = gc_ref[0].astype(F32) * h_ref[0].astype(F32)
    up = (gp_ref[0].astype(F32) * hp_ref[0].astype(F32))[HALO_ROWS - 1:HALO_ROWS]
    un = (gn_ref[0].astype(F32) * hn_ref[0].astype(F32))[0:1]
    up = jnp.where(i > 0, up, 0.0)
    un = jnp.where(i < pl.num_programs(1) - 1, un, 0.0)
    rows = lax.broadcasted_iota(jnp.int32, u.shape, 0)
    u_prev = jnp.where(rows == 0, up, pltpu.roll(u, 1, 0))
    u_next = jnp.where(rows == ts - 1, un, pltpu.roll(u, ts - 1, 0))
    w = w_ref[...]
    y = gb_ref[0].astype(F32) * (w[0:1] * u_prev + w[1:2] * u + w[2:3] * u_next)
    o_ref[0] = y.astype(o_ref.dtype)


def short_conv_branch(proj3, conv_w):
    B, S, cols = proj3.shape
    ts = min(512, S)
    c0 = (cols - 3 * CONV_CH) // CONV_CH
    hb = ts // HALO_ROWS
    n_halo = S // HALO_ROWS

    def main(c):
        return pl.BlockSpec((1, ts, CONV_CH), lambda b, i: (b, i, c))

    def prev(c):
        return pl.BlockSpec((1, HALO_ROWS, CONV_CH), lambda b, i: (b, jnp.maximum(i * hb - 1, 0), c))

    def nxt(c):
        return pl.BlockSpec((1, HALO_ROWS, CONV_CH),
                            lambda b, i: (b, jnp.minimum((i + 1) * hb, n_halo - 1), c))

    return pl.pallas_call(
        _conv_kernel,
        grid=(B, S // ts),
        in_specs=[main(c0), main(c0 + 1), main(c0 + 2), prev(c0), prev(c0 + 2), nxt(c0), nxt(c0 + 2),
                  pl.BlockSpec((3, CONV_CH), lambda b, i: (0, 0))],
        out_specs=pl.BlockSpec((1, ts, CONV_CH), lambda b, i: (b, i, 0)),
        out_shape=jax.ShapeDtypeStruct((B, S, CONV_CH), BF16),
        compiler_params=_params("parallel", "parallel"),
        name="short_conv",
    )(proj3, proj3, proj3, proj3, proj3, proj3, proj3, conv_w)


def _gate_merge_kernel(xn_ref, ya_ref, yb_ref, yc_ref, wg0_ref, wg1_ref, wg2_ref,
                       b0_ref, b1_ref, b2_ref, wa_ref, wb_ref, wc_ref, o_ref):
    xn = xn_ref[...]
    acc = jax.nn.sigmoid(_dot(xn, wg0_ref[...]) + b0_ref[...]) * _dot(ya_ref[...], wa_ref[...])
    acc += jax.nn.sigmoid(_dot(xn, wg1_ref[...]) + b1_ref[...]) * _dot(yb_ref[...], wb_ref[...])
    acc += jax.nn.sigmoid(_dot(xn, wg2_ref[...]) + b2_ref[...]) * _dot(yc_ref[...], wc_ref[...])
    o_ref[...] = acc.astype(o_ref.dtype)


def gate_merge(xn, ya, yb, yc, w_gate, b_gate, w_br):
    T, D = xn.shape
    tm = min(512, T)
    tn = min(512, D)
    nj = D // tn
    assert A_WIDTH == B_WIDTH and C_WIDTH == 2 * A_WIDTH

    def wg(k):
        return pl.BlockSpec((D, tn), lambda i, j: (0, k * nj + j))

    def bg(k):
        return pl.BlockSpec((1, tn), lambda i, j: (0, k * nj + j))

    def rowblk(w):
        return pl.BlockSpec((tm, w), lambda i, j: (i, 0))

    return pl.pallas_call(
        _gate_merge_kernel,
        grid=(T // tm, nj),
        in_specs=[rowblk(D), rowblk(A_WIDTH), rowblk(B_WIDTH), rowblk(C_WIDTH),
                  wg(0), wg(1), wg(2), bg(0), bg(1), bg(2),
                  pl.BlockSpec((A_WIDTH, tn), lambda i, j: (0, j)),
                  pl.BlockSpec((B_WIDTH, tn), lambda i, j: (1, j)),
                  pl.BlockSpec((C_WIDTH, tn), lambda i, j: (1, j))],
        out_specs=pl.BlockSpec((tm, tn), lambda i, j: (i, j)),
        out_shape=jax.ShapeDtypeStruct((T, D), BF16),
        compiler_params=_params("parallel", "arbitrary"),
        name="gate_merge",
    )(xn, ya, yb, yc, w_gate, w_gate, w_gate, b_gate, b_gate, b_gate, w_br, w_br, w_br)


def _matmul_residual_kernel(x_ref, w_ref, r_ref, o_ref):
    o_ref[...] = r_ref[...] + _dot(x_ref[...], w_ref[...])


def matmul_residual(x, w, res):
    T, K = x.shape
    N = w.shape[1]
    tm = min(1024, T)
    tn = min(1024, N)
    return pl.pallas_call(
        _matmul_residual_kernel,
        grid=(T // tm, N // tn),
        in_specs=[pl.BlockSpec((tm, K), lambda i, j: (i, 0)),
                  pl.BlockSpec((K, tn), lambda i, j: (0, j)),
                  pl.BlockSpec((tm, tn), lambda i, j: (i, j))],
        out_specs=pl.BlockSpec((tm, tn), lambda i, j: (i, j)),
        out_shape=jax.ShapeDtypeStruct((T, N), F32),
        compiler_params=_params("parallel", "arbitrary"),
        name="matmul_residual",
    )(x, w, res)


def _memkv_kernel(mem_ref, g_ref, w_ref, o_ref):
    o_ref[0, 0] = _dot(_rms(mem_ref[0], g_ref[0]).astype(BF16), w_ref[0]).astype(o_ref.dtype)


def memory_kv(mem, norm_mem, wkv):
    B, M, D = mem.shape
    L, _, N = wkv.shape
    return pl.pallas_call(
        _memkv_kernel,
        grid=(L, B),
        in_specs=[pl.BlockSpec((1, M, D), lambda l, b: (b, 0, 0)),
                  pl.BlockSpec((1, 1, D), lambda l, b: (l, 0, 0)),
                  pl.BlockSpec((1, D, N), lambda l, b: (l, 0, 0))],
        out_specs=pl.BlockSpec((1, 1, M, N), lambda l, b: (l, b, 0, 0)),
        out_shape=jax.ShapeDtypeStruct((L, B, M, N), BF16),
        compiler_params=_params("parallel", "parallel"),
        name="memory_kv",
    )(mem, norm_mem.reshape(L, 1, D), wkv)


def _cross_kernel(h_ref, gx_ref, wq_ref, kv_ref, wo_ref, gm_ref, wr_ref, hx_ref, aff_ref):
    h = h_ref[0]
    D = h.shape[1]
    q = _dot(_rms(h, gx_ref[...]).astype(BF16), wq_ref[...]) * (X_HEAD_DIM ** -0.5)
    q = q.astype(BF16)
    kv = kv_ref[0]
    outs = []
    for hd in range(X_HEADS):
        ks = slice(hd * X_HEAD_DIM, (hd + 1) * X_HEAD_DIM)
        vs = slice((X_HEADS + hd) * X_HEAD_DIM, (X_HEADS + hd + 1) * X_HEAD_DIM)
        s = _dot_nt(q[:, ks], kv[:, ks])
        e = jnp.exp(s - jnp.max(s, axis=-1, keepdims=True))
        p = e / jnp.sum(e, axis=-1, keepdims=True)
        outs.append(_dot(p.astype(BF16), kv[:, vs]))
    o = jnp.concatenate(outs, axis=1).astype(BF16)
    hn = h + _dot(o, wo_ref[...])
    xn = _rms(hn, gm_ref[...])
    hx_ref[0, :, :D] = hn
    hx_ref[0, :, D:] = xn
    lt = _dot_nt(wr_ref[...], xn, precision=lax.Precision.HIGHEST)
    e = jnp.exp(lt - jnp.max(lt, axis=0, keepdims=True))
    aff_ref[0] = e / jnp.sum(e, axis=0, keepdims=True)


def cross_attention_router(h3, g_cross, wq, kv, wo, g_moe, w_router_t):
    B, S, D = h3.shape
    M, NKV = kv.shape[1:]
    E = w_router_t.shape[0]
    NQ = wq.shape[1]
    tm = min(512, S)

    def full(shape):
        return pl.BlockSpec(shape, lambda b, i: (0,) * len(shape))

    return pl.pallas_call(
        _cross_kernel,
        grid=(B, S // tm),
        in_specs=[pl.BlockSpec((1, tm, D), lambda b, i: (b, i, 0)),
                  full((1, D)), full((D, NQ)),
                  pl.BlockSpec((1, M, NKV), lambda b, i: (b, 0, 0)),
                  full((NQ, D)), full((1, D)), full((E, D))],
        out_specs=[pl.BlockSpec((1, tm, 2 * D), lambda b, i: (b, i, 0)),
                   pl.BlockSpec((1, E, tm), lambda b, i: (b, 0, i))],
        out_shape=[jax.ShapeDtypeStruct((B, S, 2 * D), F32),
                   jax.ShapeDtypeStruct((B, E, S), F32)],
        compiler_params=_params("parallel", "parallel"),
        name="cross_attention_router",
    )(h3, g_cross.reshape(1, D), wq, kv, wo, g_moe.reshape(1, D), w_router_t)


SLOT_BLK = 128


def _select_kernel(aff_ref, cs_ref, w_ref, ts_ref, first_ref, *, cap):
    E, C, _ = aff_ref.shape[1:]
    R = E * C
    a = aff_ref[0].reshape(R, LANES)
    bits = pltpu.bitcast(a, jnp.int32)
    rid = lax.broadcasted_iota(jnp.int32, (R, LANES), 0)

    ii = lax.broadcasted_iota(jnp.int32, (LANES, LANES), 0)
    jj = lax.broadcasted_iota(jnp.int32, (LANES, LANES), 1)
    upper = jnp.where(ii <= jj, 1.0, 0.0).astype(BF16)
    ones = jnp.ones((LANES, LANES), BF16)
    ri = lax.broadcasted_iota(jnp.int32, (R, R), 0) // C
    rj = lax.broadcasted_iota(jnp.int32, (R, R), 1)
    same_expert = jnp.where(rj // C == ri, 1.0, 0.0).astype(BF16)
    rows_before = jnp.where((rj // C == ri) & (rj < lax.broadcasted_iota(jnp.int32, (R, R), 0)),
                            1.0, 0.0).astype(BF16)

    def row_totals(x):
        return _dot(x.astype(BF16), ones).astype(BF16)

    def count(x):
        return _dot(same_expert, row_totals(x))

    def prefix_count(x):
        return _dot(x.astype(BF16), upper) + _dot(rows_before, row_totals(x))

    thr = jnp.zeros((R, LANES), jnp.int32)
    for bit in range(30, -1, -1):
        cand = thr | jnp.int32(1 << bit)
        thr = jnp.where(count(jnp.where(bits >= cand, 1.0, 0.0)) >= cap, cand, thr)
    gt_f = jnp.where(bits > thr, 1.0, 0.0)
    eq_f = jnp.where(bits == thr, 1.0, 0.0)

    need = cap - count(gt_f)
    eq_rank = prefix_count(eq_f) - eq_f
    sel = gt_f + eq_f * jnp.where(eq_rank < need, 1.0, 0.0)
    csum = prefix_count(sel)
    tok = ((rid % C) * LANES + lax.broadcasted_iota(jnp.int32, (R, LANES), 1)).astype(F32)
    cs_ref[0] = csum
    w_ref[0] = a * sel
    ts_ref[0] = tok * sel
    before = _dot(rows_before, row_totals(sel))
    for blk in range(R // LANES):
        d = jnp.where(ii == jj, before[blk * LANES:(blk + 1) * LANES], 0.0)
        first_ref[0, blk:blk + 1, :] = jnp.sum(d, axis=0, keepdims=True).astype(jnp.int32)


def _resolve_kernel(first_ref, cs_ref, w_ref, ts_ref, idx_ref, gate_ref, acc_i, acc_g, *, cap):
    b = pl.program_id(0)
    e = pl.program_id(1)
    C = cs_ref.shape[1]
    n_blk = cap // SLOT_BLK
    acc_i[...] = jnp.zeros_like(acc_i)
    acc_g[...] = jnp.zeros_like(acc_g)

    def chunk(c, _):
        p0 = first_ref[b, e * C + c]
        p1 = jnp.where(c + 1 < C, first_ref[b, e * C + jnp.minimum(c + 1, C - 1)], cap)
        cs_row = cs_ref[0, pl.ds(c, 1), :]
        ts_row = ts_ref[0, pl.ds(c, 1), :]
        w_row = w_ref[0, pl.ds(c, 1), :]
        for j in range(LANES // SLOT_BLK + 1):
            blk = p0 // SLOT_BLK + j

            @pl.when((blk * SLOT_BLK < p1) & (blk < n_blk))
            def _():
                off = pl.multiple_of(blk * SLOT_BLK, SLOT_BLK)
                k1 = (lax.broadcasted_iota(jnp.int32, (SLOT_BLK, LANES), 0) + (off + 1)).astype(F32)
                hit = cs_row == k1
                rows = pl.ds(off, SLOT_BLK)
                acc_i[rows, :] = acc_i[rows, :] + jnp.where(hit, ts_row, 0.0)
                acc_g[rows, :] = acc_g[rows, :] + jnp.where(hit, w_row, 0.0)
        return 0

    lax.fori_loop(0, C, chunk, 0)
    idx_ref[0, 0] = jnp.sum(acc_i[...], axis=1, keepdims=True).astype(jnp.int32)
    gate_ref[0, 0] = jnp.sum(acc_g[...], axis=1, keepdims=True)


def expert_choice_select(aff_t, cap):
    B, E, S = aff_t.shape
    C = S // LANES
    R = E * C
    assert cap % SLOT_BLK == 0 and R % LANES == 0
    rows = pl.BlockSpec((1, R, LANES), lambda b: (b, 0, 0))
    cs, w, ts, first = pl.pallas_call(
        functools.partial(_select_kernel, cap=cap),
        grid=(B,),
        in_specs=[pl.BlockSpec((1, E, C, LANES), lambda b: (b, 0, 0, 0))],
        out_specs=[rows, rows, rows, pl.BlockSpec((1, R // LANES, LANES), lambda b: (b, 0, 0))],
        out_shape=[jax.ShapeDtypeStruct((B, R, LANES), F32)] * 3
        + [jax.ShapeDtypeStruct((B, R // LANES, LANES), jnp.int32)],
        compiler_params=_params("parallel"),
        name="expert_choice_select",
    )(aff_t.reshape(B, E, C, LANES))

    chunk_rows = pl.BlockSpec((1, C, LANES), lambda b, e, first: (b * E + e, 0, 0))
    slots = pl.BlockSpec((1, 1, cap, 1), lambda b, e, first: (b, e, 0, 0))
    return pl.pallas_call(
        functools.partial(_resolve_kernel, cap=cap),
        grid_spec=pltpu.PrefetchScalarGridSpec(
            num_scalar_prefetch=1,
            grid=(B, E),
            in_specs=[chunk_rows] * 3,
            out_specs=[slots, slots],
            scratch_shapes=[pltpu.VMEM((cap, LANES), F32)] * 2),
        out_shape=[jax.ShapeDtypeStruct((B, E, cap, 1), jnp.int32),
                   jax.ShapeDtypeStruct((B, E, cap, 1), F32)],
        compiler_params=_params("parallel", "parallel"),
        name="expert_choice_resolve",
    )(first.reshape(B, R), *[x.reshape(B * E, C, LANES) for x in (cs, w, ts)])


def _expert_ffn_kernel(idx_hbm, gate_ref, hx_in_hbm, w1_ref, w3_ref, w2_ref, hx_hbm,
                       idx_smem, buf, sems, *, sub, seq_len):
    del hx_in_hbm
    e = pl.program_id(0)
    n_exp = pl.num_programs(0)
    n_seq = gate_ref.shape[0]
    cap = idx_smem.shape[0] // n_seq
    D = w2_ref.shape[2]
    per_seq = cap // sub
    n_tiles = n_seq * per_seq
    for b in range(n_seq):
        cp = pltpu.make_async_copy(idx_hbm.at[pl.ds((b * n_exp + e) * cap, cap)],
                                   idx_smem.at[pl.ds(b * cap, cap)], sems.at[0])
        cp.start()
        cp.wait()

    def token_row(t, r):
        return pl.ds((t // per_seq) * seq_len + idx_smem[t * sub + r], 1)

    def gather(t, slot):
        def row(r, _):
            pltpu.make_async_copy(hx_hbm.at[token_row(t, r), :], buf.at[slot, pl.ds(r, 1), :],
                                  sems.at[1 + slot]).start()
            return 0
        lax.fori_loop(0, sub, row, 0, unroll=8)

    def wait_gather(slot):
        pltpu.make_async_copy(hx_hbm.at[pl.ds(0, sub), :], buf.at[slot], sems.at[1 + slot]).wait()

    def write_back(t, slot):
        def row(r, _):
            pltpu.make_async_copy(buf.at[slot, pl.ds(r, 1), pl.ds(0, D)],
                                  hx_hbm.at[token_row(t, r), pl.ds(0, D)], sems.at[3 + slot]).start()
            return 0
        lax.fori_loop(0, sub, row, 0, unroll=8)

    def wait_write_back(slot):
        pltpu.make_async_copy(buf.at[slot, :, pl.ds(0, D)], hx_hbm.at[pl.ds(0, sub), pl.ds(0, D)],
                              sems.at[3 + slot]).wait()

    def tile(t, slot):
        @pl.when(t + 1 < n_tiles)
        def _():
            @pl.when(t >= 1)
            def _():
                wait_write_back(1 - slot)
            gather(t + 1, 1 - slot)

        wait_gather(slot)
        xg = buf[slot, :, D:].astype(BF16)
        a = _dot(xg, w1_ref[0])
        c = _dot(xg, w3_ref[0])
        hid = (a * jax.nn.sigmoid(a) * c).astype(BF16)
        y = _dot(hid, w2_ref[0])
        off = pl.multiple_of((t % per_seq) * sub, sub)
        g = gate_ref[t // per_seq, 0, pl.ds(off, sub), :]
        buf[slot, :, :D] = buf[slot, :, :D] + y * g
        write_back(t, slot)

    gather(0, 0)

    def pair(p, _):
        tile(2 * p, 0)
        tile(2 * p + 1, 1)
        return 0

    lax.fori_loop(0, n_tiles // 2, pair, 0)
    wait_write_back(0)
    wait_write_back(1)


def expert_ffn(idx, gate, hx, w1, w3, w2, seq_len):
    T, D2 = hx.shape
    B, E, cap, _ = gate.shape
    D, F = w1.shape[1:]
    sub = min(256, cap)
    assert D2 == 2 * D and (B * cap // sub) % 2 == 0 and cap % 1024 == 0
    return pl.pallas_call(
        functools.partial(_expert_ffn_kernel, sub=sub, seq_len=seq_len),
        grid=(E,),
        in_specs=[pl.BlockSpec(memory_space=pl.ANY),
                  pl.BlockSpec((B, 1, cap, 1), lambda e: (0, e, 0, 0)),
                  pl.BlockSpec(memory_space=pl.ANY),
                  pl.BlockSpec((1, D, F), lambda e: (e, 0, 0)),
                  pl.BlockSpec((1, D, F), lambda e: (e, 0, 0)),
                  pl.BlockSpec((1, F, D), lambda e: (e, 0, 0))],
        out_specs=pl.BlockSpec(memory_space=pl.ANY),
        out_shape=jax.ShapeDtypeStruct((T, D2), F32),
        scratch_shapes=[pltpu.SMEM((B * cap,), jnp.int32),
                        pltpu.VMEM((2, sub, D2), F32),
                        pltpu.SemaphoreType.DMA((5,))],
        input_output_aliases={2: 0},
        compiler_params=_params("arbitrary"),
        name="expert_ffn",
    )(idx.reshape(B * E * cap), gate, hx, w1, w3, w2)


def _rope_tables(positions, dim, reps):
    inv = ROPE_THETA ** (-jnp.arange(0, dim, 2, dtype=F32) / dim)
    ang = positions.astype(F32).reshape(-1, 1) * inv
    cos, sin = jnp.cos(ang), jnp.sin(ang)
    return (jnp.tile(jnp.concatenate([cos, cos], axis=-1), (1, reps)),
            jnp.tile(jnp.concatenate([-sin, sin], axis=-1), (1, reps)))


def _by_class(a, d):
    B, S = a.shape[:2]
    return jnp.swapaxes(a.reshape(B, S // d, d, *a.shape[2:]), 1, 2)


def kernel(x, mem, positions, norm_mix, w_in, da_lambda, da_subln, conv_w, w_branch, w_gate, b_gate,
           w_out, norm_cross, norm_mem, wq_x, wkv_x, wo_x, norm_moe, w_router, w_e1, w_e3, w_e2,
           norm_final):
    B, S, D = x.shape
    depth = w_in.shape[0]
    E = w_router.shape[-1]
    cap = CAPACITY_FACTOR * S // E
    T = B * S

    ca, sa = _rope_tables(positions, DA_QK_DIM, LANES // DA_QK_DIM)
    tabs_b = [_rope_tables(_by_class(positions, d), DIL_HEAD_DIM, 1) for d in DILATIONS]
    kv_all = memory_kv(mem, norm_mem, wkv_x.astype(BF16))

    hx = x
    for l in range(depth):
        lam_init = jnp.full((1, 1), 0.8 - 0.6 * math.exp(-0.3 * l), F32)
        w_in_l = w_in[l].astype(BF16)
        xn, *xn_cls = rmsnorm_with_classes(hx, norm_mix[l])
        xn = xn.reshape(T, D)
        proj3 = in_projection(xn, w_in_l, ca, sa, *tabs_b[0]).reshape(B, S, NAT_COLS)
        ya = diff_attention(proj3, da_lambda[l], lam_init, da_subln[l]).reshape(T, A_WIDTH)
        outs, lses = [], []
        for g, (window, d) in enumerate(DIL_PAIRS):
            if d == 1:
                qkv, first = proj3.reshape(B, 1, S, NAT_COLS), B_FIRST_TILE
            else:
                xg = xn_cls[[dd for dd in DILATIONS if dd > 1].index(d)].reshape(T, D)
                qkv = in_projection(xg, w_in_l, ca, sa, *tabs_b[g], group=g)
                qkv, first = qkv.reshape(B, d, S // d, B_TILES_PER_GROUP * COL_TILE), 0
            o, lse = dilated_attention(qkv, first, window, d, g)
            outs.append(o)
            lses.append(lse)
        yb = merge_groups(outs, lses).reshape(T, B_WIDTH)
        yc = short_conv_branch(proj3, conv_w[l]).reshape(T, C_WIDTH)
        merged = gate_merge(xn, ya, yb, yc, w_gate[l].astype(BF16), b_gate[l].reshape(1, -1),
                            w_branch[l].astype(BF16))
        h = matmul_residual(merged, w_out[l].astype(BF16), hx.reshape(T, -1))
        hx, aff_t = cross_attention_router(
            h.reshape(B, S, D), norm_cross[l], wq_x[l].astype(BF16), kv_all[l],
            wo_x[l].astype(BF16), norm_moe[l], w_router[l].T)
        idx, gate = expert_choice_select(aff_t, cap)
        hx = expert_ffn(idx, gate, hx.reshape(T, 2 * D), w_e1[l].astype(BF16),
                        w_e3[l].astype(BF16), w_e2[l].astype(BF16), S).reshape(B, S, 2 * D)
    return rmsnorm(hx.reshape(T, -1), norm_final, F32).reshape(B, S, D)
```

```python
import functools
import math

import jax
import jax.numpy as jnp
from jax import lax
from jax.experimental import pallas as pl
from jax.experimental.pallas import tpu as pltpu

F32 = jnp.float32
BF16 = jnp.bfloat16

EPS = 1e-6
ROPE_THETA = 10000.0
DA_HEADS = 4
DA_QK_DIM = 64
DA_V_DIM = 2 * DA_QK_DIM
DIL_PAIRS = ((128, 1), (512, 4), (2048, 16))
DIL_HEADS = 4
DIL_HEAD_DIM = 128
CONV_CH = 1024
A_WIDTH = DA_HEADS * DA_V_DIM
B_WIDTH = DIL_HEADS * DIL_HEAD_DIM
C_WIDTH = CONV_CH
MIX_WIDTH = A_WIDTH + B_WIDTH + C_WIDTH
N_BRANCH = 3
IN_COLS = 3 * A_WIDTH + len(DIL_PAIRS) * 3 * B_WIDTH + 3 * CONV_CH
X_HEADS = 4
X_HEAD_DIM = 128
CAPACITY_FACTOR = 2

LANES = 128
V7X_VMEM_BYTES = 64 * 1024 * 1024
VMEM_LIMIT = V7X_VMEM_BYTES - 8 * 1024 * 1024
COL_TILE = 512
BF16_ROWS = 16
DIL_QBLK = 256
NEG_BIG = -1e30


def _params(*sem):
    return pltpu.CompilerParams(dimension_semantics=sem, vmem_limit_bytes=VMEM_LIMIT)


def _rms(x, g):
    return x * lax.rsqrt(jnp.mean(x * x, axis=-1, keepdims=True) + EPS) * g


def _dot(a, b):
    return jnp.dot(a, b, preferred_element_type=F32)


def _dot_nt(a, b, **kw):
    return lax.dot_general(a, b, (((1,), (1,)), ((), ())), preferred_element_type=F32, **kw)


def _rms_kernel(x_ref, g_ref, o_ref):
    o_ref[...] = _rms(x_ref[...], g_ref[...]).astype(o_ref.dtype)


def rmsnorm(x, g, out_dtype):
    T = x.shape[0]
    D = g.shape[0]
    tm = min(512, T)
    return pl.pallas_call(
        _rms_kernel,
        grid=(T // tm,),
        in_specs=[pl.BlockSpec((tm, D), lambda i: (i, 0)),
                  pl.BlockSpec((1, D), lambda i: (0, 0))],
        out_specs=pl.BlockSpec((tm, D), lambda i: (i, 0)),
        out_shape=jax.ShapeDtypeStruct((T, D), out_dtype),
        compiler_params=_params("parallel"),
        name="rmsnorm",
    )(x, g.reshape(1, D))


DILATIONS = tuple(d for _, d in DIL_PAIRS)


def _rms_classes_kernel(x_ref, g_ref, o_ref, *rest):
    class_refs, scr = rest[:-1], rest[-1]
    y = _rms(x_ref[0], g_ref[...])
    o_ref[0] = y.astype(o_ref.dtype)
    tm, D = y.shape
    for c in range(D // LANES):
        cs = slice(c * LANES, (c + 1) * LANES)
        scr[c] = y[:, cs]
        for ref, d in zip(class_refs, [d for d in DILATIONS if d > 1]):
            for r in range(d):
                ref[0, r, :, cs] = scr[c, pl.ds(r, tm // d, stride=d), :].astype(ref.dtype)


def rmsnorm_with_classes(x3, g):
    B, S, _ = x3.shape
    D = g.shape[0]
    tm = min(512, S)
    ds = [d for d in DILATIONS if d > 1]
    return pl.pallas_call(
        _rms_classes_kernel,
        grid=(B, S // tm),
        in_specs=[pl.BlockSpec((1, tm, D), lambda b, i: (b, i, 0)),
                  pl.BlockSpec((1, D), lambda b, i: (0, 0))],
        out_specs=[pl.BlockSpec((1, tm, D), lambda b, i: (b, i, 0))]
        + [pl.BlockSpec((1, d, tm // d, D), lambda b, i: (b, 0, i, 0)) for d in ds],
        out_shape=[jax.ShapeDtypeStruct((B, S, D), BF16)]
        + [jax.ShapeDtypeStruct((B, d, S // d, D), BF16) for d in ds],
        scratch_shapes=[pltpu.VMEM((D // LANES, tm, LANES), F32)],
        compiler_params=_params("parallel", "parallel"),
        name="rmsnorm_classes",
    )(x3, g.reshape(1, D))


N_COL_TILES = IN_COLS // COL_TILE
A_ROPE_TILES = 2 * A_WIDTH // COL_TILE
B_FIRST_TILE = 3 * A_WIDTH // COL_TILE
B_TILES_PER_GROUP = 3 * B_WIDTH // COL_TILE
B_ROPE_PER_GROUP = 2 * B_WIDTH // COL_TILE
C_FIRST_TILE = B_FIRST_TILE + len(DIL_PAIRS) * B_TILES_PER_GROUP


def _inproj_kernel(x_ref, w_ref, ca_ref, sa_ref, cb_ref, sb_ref, o_ref, *, col_map):
    j = col_map(pl.program_id(1))
    acc = _dot(x_ref[...], w_ref[...])
    jb = j - B_FIRST_TILE
    is_a = j < A_ROPE_TILES
    is_b = (j >= B_FIRST_TILE) & (j < C_FIRST_TILE) & (jb % B_TILES_PER_GROUP < B_ROPE_PER_GROUP)

    @pl.when(is_a)
    def _():
        c, s = ca_ref[...], sa_ref[...]
        lane = lax.broadcasted_iota(jnp.int32, c.shape, 1)
        first = (lane % DA_QK_DIM) < (DA_QK_DIM // 2)
        for t in range(COL_TILE // LANES):
            x = acc[:, t * LANES:(t + 1) * LANES]
            rot = jnp.where(first, pltpu.roll(x, LANES - DA_QK_DIM // 2, 1),
                            pltpu.roll(x, DA_QK_DIM // 2, 1))
            o_ref[:, t * LANES:(t + 1) * LANES] = (x * c + rot * s).astype(o_ref.dtype)

    @pl.when(is_b)
    def _():
        c, s = cb_ref[...], sb_ref[...]
        for t in range(COL_TILE // LANES):
            x = acc[:, t * LANES:(t + 1) * LANES]
            rot = pltpu.roll(x, DIL_HEAD_DIM // 2, 1)
            o_ref[:, t * LANES:(t + 1) * LANES] = (x * c + rot * s).astype(o_ref.dtype)

    @pl.when(jnp.logical_not(is_a | is_b))
    def _():
        o_ref[...] = acc.astype(o_ref.dtype)


NAT_TILES = N_COL_TILES - (len(DIL_PAIRS) - 1) * B_TILES_PER_GROUP
NAT_COLS = NAT_TILES * COL_TILE
NAT_C_FIRST_TILE = B_FIRST_TILE + B_TILES_PER_GROUP


def _natural_col_map(j):
    return j + jnp.where(j >= NAT_C_FIRST_TILE, (len(DIL_PAIRS) - 1) * B_TILES_PER_GROUP, 0)


def in_projection(xn, w, ca, sa, cb, sb, group=None):
    T, D = xn.shape
    tm = min(2048, T)
    if group is None:
        n_tiles, col_map = NAT_TILES, _natural_col_map
    else:
        first = B_FIRST_TILE + group * B_TILES_PER_GROUP
        n_tiles, col_map = B_TILES_PER_GROUP, lambda j: j + first
    tab = pl.BlockSpec((tm, LANES), lambda i, j: (i, 0))
    return pl.pallas_call(
        functools.partial(_inproj_kernel, col_map=col_map),
        grid=(T // tm, n_tiles),
        in_specs=[pl.BlockSpec((tm, D), lambda i, j: (i, 0)),
                  pl.BlockSpec((D, COL_TILE), lambda i, j: (0, col_map(j))),
                  tab, tab, tab, tab],
        out_specs=pl.BlockSpec((tm, COL_TILE), lambda i, j: (i, j)),
        out_shape=jax.ShapeDtypeStruct((T, n_tiles * COL_TILE), BF16),
        compiler_params=_params("parallel", "arbitrary"),
        name="in_projection" if group is None else f"in_projection_g{group}",
    )(xn, w, ca, sa, cb, sb)


KV_UNROLL = 8


def _diff_attn_kernel(q_ref, k_ref, v_ref, lam_ref, li_ref, g_ref, o_ref, vext_ref, *, tk):
    S = k_ref.shape[1]
    tq = q_ref.shape[1]

    @pl.when(pl.program_id(2) == 0)
    def _():
        vext_ref[:, :DA_V_DIM] = v_ref[0]
        vext_ref[:, DA_V_DIM:] = jnp.ones((S, DA_V_DIM), BF16)

    q = q_ref[0] * jnp.asarray(DA_QK_DIM ** -0.5, BF16)
    lane = lax.broadcasted_iota(jnp.int32, q.shape, 1)
    zero = jnp.zeros_like(q)
    q0 = jnp.where(lane < DA_QK_DIM, q, zero)
    q1 = jnp.where(lane >= DA_QK_DIM, q, zero)

    def step(qm, kj, vj, m, a):
        s = _dot_nt(qm, kj)
        m_new = jnp.maximum(m, jnp.max(s, axis=-1, keepdims=True))
        p = jnp.exp((s - m_new).astype(BF16))
        a = jnp.exp(m - m_new) * a + _dot(p, vj)
        return m_new, a

    def body(j, carry):
        m0, a0, m1, a1 = carry
        off = pl.multiple_of(j * tk, tk)
        kj = k_ref[0, pl.ds(off, tk), :]
        vj = vext_ref[pl.ds(off, tk), :]
        m0, a0 = step(q0, kj, vj, m0, a0)
        m1, a1 = step(q1, kj, vj, m1, a1)
        return m0, a0, m1, a1

    mi = jnp.full((tq, 1), NEG_BIG, F32)
    ai = jnp.zeros((tq, 2 * DA_V_DIM), F32)
    _, a0, _, a1 = lax.fori_loop(0, S // tk, body, (mi, ai, mi, ai), unroll=KV_UNROLL)

    lp = lam_ref[...]
    lam_init = li_ref[...]
    lam = (jnp.exp(jnp.sum(lp[0:1] * lp[1:2], axis=-1, keepdims=True))
           - jnp.exp(jnp.sum(lp[2:3] * lp[3:4], axis=-1, keepdims=True)) + lam_init)
    o = a0[:, :DA_V_DIM] / a0[:, DA_V_DIM:] - lam * (a1[:, :DA_V_DIM] / a1[:, DA_V_DIM:])
    o_ref[0] = (_rms(o, g_ref[...]) * (1.0 - lam_init)).astype(o_ref.dtype)


def diff_attention(proj3, lam_p, lam_init, sub_gain):
    B, S, _ = proj3.shape
    tq = min(1024, S)
    tk = min(512, S)
    assert (S // tk) % KV_UNROLL == 0
    kb = A_WIDTH // DA_V_DIM
    return pl.pallas_call(
        functools.partial(_diff_attn_kernel, tk=tk),
        grid=(B, DA_HEADS, S // tq),
        in_specs=[pl.BlockSpec((1, tq, DA_V_DIM), lambda b, h, i: (b, i, h)),
                  pl.BlockSpec((1, S, DA_V_DIM), lambda b, h, i: (b, 0, kb + h)),
                  pl.BlockSpec((1, S, DA_V_DIM), lambda b, h, i: (b, 0, 2 * kb + h)),
                  pl.BlockSpec((4, DA_QK_DIM), lambda b, h, i: (0, 0)),
                  pl.BlockSpec((1, 1), lambda b, h, i: (0, 0)),
                  pl.BlockSpec((1, DA_V_DIM), lambda b, h, i: (0, 0))],
        out_specs=pl.BlockSpec((1, tq, DA_V_DIM), lambda b, h, i: (b, i, h)),
        out_shape=jax.ShapeDtypeStruct((B, S, A_WIDTH), BF16),
        scratch_shapes=[pltpu.VMEM((S, 2 * DA_V_DIM), BF16)],
        compiler_params=_params("parallel", "parallel", "arbitrary"),
        name="diff_attention",
    )(proj3, proj3, proj3, lam_p, lam_init, sub_gain.reshape(1, DA_V_DIM))


def _dilated_kernel(q_ref, k_ref, v_ref, o_ref, lse_ref, *, n_side):
    i = pl.program_id(2)
    tq = q_ref.shape[2]
    L = k_ref.shape[2]
    W = tq + 2 * n_side
    start = pl.multiple_of(jnp.clip(i * tq - n_side, 0, L - W), n_side)
    qpos = i * tq + lax.broadcasted_iota(jnp.int32, (tq, W), 0)
    kpos = start + lax.broadcasted_iota(jnp.int32, (tq, W), 1)
    valid = jnp.abs(kpos - qpos) <= n_side
    scale = DIL_HEAD_DIM ** -0.5
    for h in range(DIL_HEADS):
        hs = slice(h * DIL_HEAD_DIM, (h + 1) * DIL_HEAD_DIM)
        kw = k_ref[0, 0, pl.ds(start, W), hs]
        vw = v_ref[0, 0, pl.ds(start, W), hs]
        s = jnp.where(valid, _dot_nt(q_ref[0, 0, :, hs], kw) * scale, NEG_BIG)
        m = jnp.max(s, axis=-1, keepdims=True)
        e = jnp.exp(s - m)
        l = jnp.sum(e, axis=-1, keepdims=True)
        o_ref[0, 0, :, hs] = _dot((e / l).astype(BF16), vw)
        lse_ref[0, 0, :, hs] = jnp.broadcast_to(m + jnp.log(l), (tq, DIL_HEAD_DIM))


def dilated_attention(qkv, first_tile, window, dilation, group):
    B, d, L, _ = qkv.shape
    n_side = (window // 2) // dilation
    tq = next(t for t in (DIL_QBLK, DIL_QBLK // 2) if t + 2 * n_side <= L and L % t == 0)
    assert d == dilation and n_side % BF16_ROWS == 0

    def resident(kind):
        return pl.BlockSpec((1, 1, L, B_WIDTH), lambda b, r, i: (b, r, 0, first_tile + kind))

    out_spec = pl.BlockSpec((1, 1, tq, B_WIDTH), lambda b, r, i: (b, r, i, 0))
    return pl.pallas_call(
        functools.partial(_dilated_kernel, n_side=n_side),
        grid=(B, d, L // tq),
        in_specs=[pl.BlockSpec((1, 1, tq, B_WIDTH), lambda b, r, i: (b, r, i, first_tile)),
                  resident(1), resident(2)],
        out_specs=[out_spec, out_spec],
        out_shape=[jax.ShapeDtypeStruct((B, d, L, B_WIDTH), F32)] * 2,
        compiler_params=_params("parallel", "parallel", "arbitrary"),
        name=f"dilated_attention_g{group}",
    )(qkv, qkv, qkv)


def _merge_kernel(*refs):
    n = len(DILATIONS)
    ins, y_ref, scr = refs[:2 * n], refs[2 * n], refs[2 * n + 1:]
    tm = y_ref.shape[1]
    for c in range(B_WIDTH // LANES):
        cs = slice(c * LANES, (c + 1) * LANES)
        vals, k = [], 0
        for g, d in enumerate(DILATIONS):
            o_ref, l_ref = ins[2 * g], ins[2 * g + 1]
            if d == 1:
                vals.append((o_ref[0, 0, :, cs], l_ref[0, 0, :, cs]))
                continue
            so, sl = scr[2 * k], scr[2 * k + 1]
            k += 1
            for r in range(d):
                so[c, pl.ds(r, tm // d, stride=d), :] = o_ref[0, r, :, cs]
                sl[c, pl.ds(r, tm // d, stride=d), :] = l_ref[0, r, :, cs]
            vals.append((so[c], sl[c]))
        m = functools.reduce(jnp.maximum, [l for _, l in vals])
        ws = [jnp.exp(l - m) for _, l in vals]
        num = sum(w * o for w, (o, _) in zip(ws, vals))
        y_ref[0, :, cs] = (num / sum(ws)).astype(y_ref.dtype)


def merge_groups(outs, lses):
    B, _, S, _ = outs[0].shape
    tm = min(512, S)
    specs, args = [], []
    for d, o, l in zip(DILATIONS, outs, lses):
        sp = pl.BlockSpec((1, d, tm // d, B_WIDTH), lambda b, i: (b, 0, i, 0))
        specs += [sp, sp]
        args += [o, l]
    n_scr = 2 * sum(1 for d in DILATIONS if d > 1)
    return pl.pallas_call(
        _merge_kernel,
        grid=(B, S // tm),
        in_specs=specs,
        out_specs=pl.BlockSpec((1, tm, B_WIDTH), lambda b, i: (b, i, 0)),
        out_shape=jax.ShapeDtypeStruct((B, S, B_WIDTH), BF16),
        scratch_shapes=[pltpu.VMEM((B_WIDTH // LANES, tm, LANES), F32)] * n_scr,
        compiler_params=_params("parallel", "parallel"),
        name="merge_groups",
    )(*args)


HALO_ROWS = 16


def _conv_kernel(h_ref, gb_ref, gc_ref, hp_ref, gp_ref, hn_ref, gn_ref, w_ref, o_ref):
    i = pl.program_id(1)
    ts = h_ref.shape[1]
    u = gc_ref[0].astype(F32) * h_ref[0].astype(F32)
    up = (gp_ref[0].astype(F32) * hp_ref[0].astype(F32))[HALO_ROWS - 1:HALO_ROWS]
    un = (gn_ref[0].astype(F32) * hn_ref[0].astype(F32))[0:1]
    up = jnp.where(i > 0, up, 0.0)
    un = jnp.where(i < pl.num_programs(1) - 1, un, 0.0)
    rows = lax.broadcasted_iota(jnp.int32, u.shape, 0)
    u_prev = jnp.where(rows == 0, up, pltpu.roll(u, 1, 0))
    u_next = jnp.where(rows == ts - 1, un, pltpu.roll(u, ts - 1, 0))
    w = w_ref[...]
    y = gb_ref[0].astype(F32) * (w[0:1] * u_prev + w[1:2] * u + w[2:3] * u_next)
    o_ref[0] = y.astype(o_ref.dtype)


def short_conv_branch(proj3, conv_w):
    B, S, cols = proj3.shape
    ts = min(512, S)
    c0 = (cols - 3 * CONV_CH) // CONV_CH
    hb = ts // HALO_ROWS
    n_halo = S // HALO_ROWS

    def main(c):
        return pl.BlockSpec((1, ts, CONV_CH), lambda b, i: (b, i, c))

    def prev(c):
        return pl.BlockSpec((1, HALO_ROWS, CONV_CH), lambda b, i: (b, jnp.maximum(i * hb - 1, 0), c))

    def nxt(c):
        return pl.BlockSpec((1, HALO_ROWS, CONV_CH),
                            lambda b, i: (b, jnp.minimum((i + 1) * hb, n_halo - 1), c))

    return pl.pallas_call(
        _conv_kernel,
        grid=(B, S // ts),
        in_specs=[main(c0), main(c0 + 1), main(c0 + 2), prev(c0), prev(c0 + 2), nxt(c0), nxt(c0 + 2),
                  pl.BlockSpec((3, CONV_CH), lambda b, i: (0, 0))],
        out_specs=pl.BlockSpec((1, ts, CONV_CH), lambda b, i: (b, i, 0)),
        out_shape=jax.ShapeDtypeStruct((B, S, CONV_CH), BF16),
        compiler_params=_params("parallel", "parallel"),
        name="short_conv",
    )(proj3, proj3, proj3, proj3, proj3, proj3, proj3, conv_w)


def _gate_merge_kernel(xn_ref, ya_ref, yb_ref, yc_ref, wg0_ref, wg1_ref, wg2_ref,
                       b0_ref, b1_ref, b2_ref, wa_ref, wb_ref, wc_ref, o_ref):
    xn = xn_ref[...]
    acc = jax.nn.sigmoid(_dot(xn, wg0_ref[...]) + b0_ref[...]) * _dot(ya_ref[...], wa_ref[...])
    acc += jax.nn.sigmoid(_dot(xn, wg1_ref[...]) + b1_ref[...]) * _dot(yb_ref[...], wb_ref[...])
    acc += jax.nn.sigmoid(_dot(xn, wg2_ref[...]) + b2_ref[...]) * _dot(yc_ref[...], wc_ref[...])
    o_ref[...] = acc.astype(o_ref.dtype)


def gate_merge(xn, ya, yb, yc, w_gate, b_gate, w_br):
    T, D = xn.shape
    tm = min(512, T)
    tn = min(512, D)
    nj = D // tn
    assert A_WIDTH == B_WIDTH and C_WIDTH == 2 * A_WIDTH

    def wg(k):
        return pl.BlockSpec((D, tn), lambda i, j: (0, k * nj + j))

    def bg(k):
        return pl.BlockSpec((1, tn), lambda i, j: (0, k * nj + j))

    def rowblk(w):
        return pl.BlockSpec((tm, w), lambda i, j: (i, 0))

    return pl.pallas_call(
        _gate_merge_kernel,
        grid=(T // tm, nj),
        in_specs=[rowblk(D), rowblk(A_WIDTH), rowblk(B_WIDTH), rowblk(C_WIDTH),
                  wg(0), wg(1), wg(2), bg(0), bg(1), bg(2),
                  pl.BlockSpec((A_WIDTH, tn), lambda i, j: (0, j)),
                  pl.BlockSpec((B_WIDTH, tn), lambda i, j: (1, j)),
                  pl.BlockSpec((C_WIDTH, tn), lambda i, j: (1, j))],
        out_specs=pl.BlockSpec((tm, tn), lambda i, j: (i, j)),
        out_shape=jax.ShapeDtypeStruct((T, D), BF16),
        compiler_params=_params("parallel", "arbitrary"),
        name="gate_merge",
    )(xn, ya, yb, yc, w_gate, w_gate, w_gate, b_gate, b_gate, b_gate, w_br, w_br, w_br)


def _matmul_residual_kernel(x_ref, w_ref, r_ref, o_ref):
    o_ref[...] = r_ref[...] + _dot(x_ref[...], w_ref[...])


def matmul_residual(x, w, res):
    T, K = x.shape
    N = w.shape[1]
    tm = min(1024, T)
    tn = min(1024, N)
    return pl.pallas_call(
        _matmul_residual_kernel,
        grid=(T // tm, N // tn),
        in_specs=[pl.BlockSpec((tm, K), lambda i, j: (i, 0)),
                  pl.BlockSpec((K, tn), lambda i, j: (0, j)),
                  pl.BlockSpec((tm, tn), lambda i, j: (i, j))],
        out_specs=pl.BlockSpec((tm, tn), lambda i, j: (i, j)),
        out_shape=jax.ShapeDtypeStruct((T, N), F32),
        compiler_params=_params("parallel", "arbitrary"),
        name="matmul_residual",
    )(x, w, res)


def _memkv_kernel(mem_ref, g_ref, w_ref, o_ref):
    o_ref[0, 0] = _dot(_rms(mem_ref[0], g_ref[0]).astype(BF16), w_ref[0]).astype(o_ref.dtype)


def memory_kv(mem, norm_mem, wkv):
    B, M, D = mem.shape
    L, _, N = wkv.shape
    return pl.pallas_call(
        _memkv_kernel,
        grid=(L, B),
        in_specs=[pl.BlockSpec((1, M, D), lambda l, b: (b, 0, 0)),
                  pl.BlockSpec((1, 1, D), lambda l, b: (l, 0, 0)),
                  pl.BlockSpec((1, D, N), lambda l, b: (l, 0, 0))],
        out_specs=pl.BlockSpec((1, 1, M, N), lambda l, b: (l, b, 0, 0)),
        out_shape=jax.ShapeDtypeStruct((L, B, M, N), BF16),
        compiler_params=_params("parallel", "parallel"),
        name="memory_kv",
    )(mem, norm_mem.reshape(L, 1, D), wkv)


def _pack_bf16_pairs(x):
    m = x.shape[1] // 2
    hi = pltpu.bitcast(x[:, :m].astype(BF16).astype(F32), jnp.uint32)
    lo = pltpu.bitcast(x[:, m:].astype(BF16).astype(F32), jnp.uint32)
    return pltpu.bitcast(hi | (lo >> 16), F32)


def _unpack_bf16_pairs(w):
    bits = pltpu.bitcast(w, jnp.uint32)
    hi = pltpu.bitcast(bits & jnp.uint32(0xFFFF0000), F32).astype(BF16)
    lo = pltpu.bitcast(bits << 16, F32).astype(BF16)
    return jnp.concatenate([hi, lo], axis=1)


def _cross_kernel(h_ref, gx_ref, wq_ref, kv_ref, wo_ref, gm_ref, wr_ref, hx_ref, aff_ref):
    h = h_ref[0]
    D = h.shape[1]
    q = _dot(_rms(h, gx_ref[...]).astype(BF16), wq_ref[...]) * (X_HEAD_DIM ** -0.5)
    q = q.astype(BF16)
    kv = kv_ref[0]
    outs = []
    for hd in range(X_HEADS):
        ks = slice(hd * X_HEAD_DIM, (hd + 1) * X_HEAD_DIM)
        vs = slice((X_HEADS + hd) * X_HEAD_DIM, (X_HEADS + hd + 1) * X_HEAD_DIM)
        s = _dot_nt(q[:, ks], kv[:, ks])
        e = jnp.exp(s - jnp.max(s, axis=-1, keepdims=True))
        p = e / jnp.sum(e, axis=-1, keepdims=True)
        outs.append(_dot(p.astype(BF16), kv[:, vs]))
    o = jnp.concatenate(outs, axis=1).astype(BF16)
    hn = h + _dot(o, wo_ref[...])
    xn = _rms(hn, gm_ref[...])
    hx_ref[0, :, :D] = hn
    hx_ref[0, :, D:] = _pack_bf16_pairs(xn)
    lt = _dot_nt(wr_ref[...], xn, precision=lax.Precision.HIGHEST)
    e = jnp.exp(lt - jnp.max(lt, axis=0, keepdims=True))
    aff_ref[0] = e / jnp.sum(e, axis=0, keepdims=True)


def cross_attention_router(h3, g_cross, wq, kv, wo, g_moe, w_router_t):
    B, S, D = h3.shape
    M, NKV = kv.shape[1:]
    E = w_router_t.shape[0]
    NQ = wq.shape[1]
    tm = min(512, S)

    def full(shape):
        return pl.BlockSpec(shape, lambda b, i: (0,) * len(shape))

    return pl.pallas_call(
        _cross_kernel,
        grid=(B, S // tm),
        in_specs=[pl.BlockSpec((1, tm, D), lambda b, i: (b, i, 0)),
                  full((1, D)), full((D, NQ)),
                  pl.BlockSpec((1, M, NKV), lambda b, i: (b, 0, 0)),
                  full((NQ, D)), full((1, D)), full((E, D))],
        out_specs=[pl.BlockSpec((1, tm, D + D // 2), lambda b, i: (b, i, 0)),
                   pl.BlockSpec((1, E, tm), lambda b, i: (b, 0, i))],
        out_shape=[jax.ShapeDtypeStruct((B, S, D + D // 2), F32),
                   jax.ShapeDtypeStruct((B, E, S), F32)],
        compiler_params=_params("parallel", "parallel"),
        name="cross_attention_router",
    )(h3, g_cross.reshape(1, D), wq, kv, wo, g_moe.reshape(1, D), w_router_t)


SLOT_BLK = 128


def _select_kernel(aff_ref, cs_ref, w_ref, ts_ref, first_ref, *, cap):
    E, C, _ = aff_ref.shape[1:]
    R = E * C
    a = aff_ref[0].reshape(R, LANES)
    bits = pltpu.bitcast(a, jnp.int32)
    rid = lax.broadcasted_iota(jnp.int32, (R, LANES), 0)

    ii = lax.broadcasted_iota(jnp.int32, (LANES, LANES), 0)
    jj = lax.broadcasted_iota(jnp.int32, (LANES, LANES), 1)
    upper = jnp.where(ii <= jj, 1.0, 0.0).astype(BF16)
    ones = jnp.ones((LANES, LANES), BF16)
    ri = lax.broadcasted_iota(jnp.int32, (R, R), 0) // C
    rj = lax.broadcasted_iota(jnp.int32, (R, R), 1)
    same_expert = jnp.where(rj // C == ri, 1.0, 0.0).astype(BF16)
    rows_before = jnp.where((rj // C == ri) & (rj < lax.broadcasted_iota(jnp.int32, (R, R), 0)),
                            1.0, 0.0).astype(BF16)

    def row_totals(x):
        return _dot(x.astype(BF16), ones).astype(BF16)

    def count(x):
        return _dot(same_expert, row_totals(x))

    def prefix_count(x):
        return _dot(x.astype(BF16), upper) + _dot(rows_before, row_totals(x))

    thr = jnp.zeros((R, LANES), jnp.int32)
    for bit in range(30, -1, -1):
        cand = thr | jnp.int32(1 << bit)
        thr = jnp.where(count(jnp.where(bits >= cand, 1.0, 0.0)) >= cap, cand, thr)
    gt_f = jnp.where(bits > thr, 1.0, 0.0)
    eq_f = jnp.where(bits == thr, 1.0, 0.0)

    need = cap - count(gt_f)
    eq_rank = prefix_count(eq_f) - eq_f
    sel = gt_f + eq_f * jnp.where(eq_rank < need, 1.0, 0.0)
    csum = prefix_count(sel)
    tok = ((rid % C) * LANES + lax.broadcasted_iota(jnp.int32, (R, LANES), 1)).astype(F32)
    cs_ref[0] = csum
    w_ref[0] = a * sel
    ts_ref[0] = tok * sel
    before = _dot(rows_before, row_totals(sel))
    for blk in range(R // LANES):
        d = jnp.where(ii == jj, before[blk * LANES:(blk + 1) * LANES], 0.0)
        first_ref[0, blk:blk + 1, :] = jnp.sum(d, axis=0, keepdims=True).astype(jnp.int32)


def _resolve_kernel(first_ref, cs_ref, w_ref, ts_ref, idx_ref, gate_ref, acc_i, acc_g, *, cap):
    b = pl.program_id(0)
    e = pl.program_id(1)
    C = cs_ref.shape[1]
    n_blk = cap // SLOT_BLK
    acc_i[...] = jnp.zeros_like(acc_i)
    acc_g[...] = jnp.zeros_like(acc_g)

    def chunk(c, _):
        p0 = first_ref[b, e * C + c]
        p1 = jnp.where(c + 1 < C, first_ref[b, e * C + jnp.minimum(c + 1, C - 1)], cap)
        cs_row = cs_ref[0, pl.ds(c, 1), :]
        ts_row = ts_ref[0, pl.ds(c, 1), :]
        w_row = w_ref[0, pl.ds(c, 1), :]
        for j in range(LANES // SLOT_BLK + 1):
            blk = p0 // SLOT_BLK + j

            @pl.when((blk * SLOT_BLK < p1) & (blk < n_blk))
            def _():
                off = pl.multiple_of(blk * SLOT_BLK, SLOT_BLK)
                k1 = (lax.broadcasted_iota(jnp.int32, (SLOT_BLK, LANES), 0) + (off + 1)).astype(F32)
                hit = cs_row == k1
                rows = pl.ds(off, SLOT_BLK)
                acc_i[rows, :] = acc_i[rows, :] + jnp.where(hit, ts_row, 0.0)
                acc_g[rows, :] = acc_g[rows, :] + jnp.where(hit, w_row, 0.0)
        return 0

    lax.fori_loop(0, C, chunk, 0)
    idx_ref[0, 0] = jnp.sum(acc_i[...], axis=1, keepdims=True).astype(jnp.int32)
    gate_ref[0, 0] = jnp.sum(acc_g[...], axis=1, keepdims=True)


def expert_choice_select(aff_t, cap):
    B, E, S = aff_t.shape
    C = S // LANES
    R = E * C
    assert cap % SLOT_BLK == 0 and R % LANES == 0
    rows = pl.BlockSpec((1, R, LANES), lambda b: (b, 0, 0))
    cs, w, ts, first = pl.pallas_call(
        functools.partial(_select_kernel, cap=cap),
        grid=(B,),
        in_specs=[pl.BlockSpec((1, E, C, LANES), lambda b: (b, 0, 0, 0))],
        out_specs=[rows, rows, rows, pl.BlockSpec((1, R // LANES, LANES), lambda b: (b, 0, 0))],
        out_shape=[jax.ShapeDtypeStruct((B, R, LANES), F32)] * 3
        + [jax.ShapeDtypeStruct((B, R // LANES, LANES), jnp.int32)],
        compiler_params=_params("parallel"),
        name="expert_choice_select",
    )(aff_t.reshape(B, E, C, LANES))

    chunk_rows = pl.BlockSpec((1, C, LANES), lambda b, e, first: (b * E + e, 0, 0))
    slots = pl.BlockSpec((1, 1, cap, 1), lambda b, e, first: (b, e, 0, 0))
    return pl.pallas_call(
        functools.partial(_resolve_kernel, cap=cap),
        grid_spec=pltpu.PrefetchScalarGridSpec(
            num_scalar_prefetch=1,
            grid=(B, E),
            in_specs=[chunk_rows] * 3,
            out_specs=[slots, slots],
            scratch_shapes=[pltpu.VMEM((cap, LANES), F32)] * 2),
        out_shape=[jax.ShapeDtypeStruct((B, E, cap, 1), jnp.int32),
                   jax.ShapeDtypeStruct((B, E, cap, 1), F32)],
        compiler_params=_params("parallel", "parallel"),
        name="expert_choice_resolve",
    )(first.reshape(B, R), *[x.reshape(B * E, C, LANES) for x in (cs, w, ts)])


def _expert_ffn_kernel(idx_hbm, gate_ref, hx_in_hbm, w1_ref, w3_ref, w2_ref, hx_hbm,
                       idx_smem, buf, sems, *, sub, seq_len):
    del hx_in_hbm
    e = pl.program_id(0)
    n_exp = pl.num_programs(0)
    n_seq = gate_ref.shape[0]
    cap = idx_smem.shape[0] // n_seq
    D = w2_ref.shape[2]
    per_seq = cap // sub
    n_tiles = n_seq * per_seq
    for b in range(n_seq):
        cp = pltpu.make_async_copy(idx_hbm.at[pl.ds((b * n_exp + e) * cap, cap)],
                                   idx_smem.at[pl.ds(b * cap, cap)], sems.at[0])
        cp.start()
        cp.wait()

    def token_row(t, r):
        return pl.ds((t // per_seq) * seq_len + idx_smem[t * sub + r], 1)

    def gather(t, slot):
        for r in range(sub):
            pltpu.make_async_copy(hx_hbm.at[token_row(t, r), :], buf.at[slot, pl.ds(r, 1), :],
                                  sems.at[1 + slot]).start()

    def wait_gather(slot):
        pltpu.make_async_copy(hx_hbm.at[pl.ds(0, sub), :], buf.at[slot], sems.at[1 + slot]).wait()

    def write_back(t, slot):
        for r in range(sub):
            pltpu.make_async_copy(buf.at[slot, pl.ds(r, 1), pl.ds(0, D)],
                                  hx_hbm.at[token_row(t, r), pl.ds(0, D)], sems.at[3 + slot]).start()

    def wait_write_back(slot):
        pltpu.make_async_copy(buf.at[slot, :, pl.ds(0, D)], hx_hbm.at[pl.ds(0, sub), pl.ds(0, D)],
                              sems.at[3 + slot]).wait()

    def tile(t, slot, last):
        wait_gather(slot)
        if not last:
            @pl.when(t >= 1)
            def _():
                wait_write_back(1 - slot)
            gather(t + 1, 1 - slot)
        xg = _unpack_bf16_pairs(buf[slot, :, D:])
        a = _dot(xg, w1_ref[0])
        c = _dot(xg, w3_ref[0])
        hid = (a * jax.nn.sigmoid(a) * c).astype(BF16)
        y = _dot(hid, w2_ref[0])
        off = pl.multiple_of((t % per_seq) * sub, sub)
        g = gate_ref[t // per_seq, 0, pl.ds(off, sub), :]
        buf[slot, :, :D] = buf[slot, :, :D] + y * g
        write_back(t, slot)

    gather(0, 0)

    def pair(p, _):
        tile(2 * p, 0, False)
        tile(2 * p + 1, 1, False)
        return 0

    lax.fori_loop(0, n_tiles // 2 - 1, pair, 0)
    tile(n_tiles - 2, 0, False)
    tile(n_tiles - 1, 1, True)
    wait_write_back(0)
    wait_write_back(1)


def expert_ffn(idx, gate, hx, w1, w3, w2, seq_len):
    T, D2 = hx.shape
    B, E, cap, _ = gate.shape
    D, F = w1.shape[1:]
    sub = min(256, cap)
    assert D2 == D + D // 2 and (B * cap // sub) % 2 == 0 and cap % 1024 == 0
    return pl.pallas_call(
        functools.partial(_expert_ffn_kernel, sub=sub, seq_len=seq_len),
        grid=(E,),
        in_specs=[pl.BlockSpec(memory_space=pl.ANY),
                  pl.BlockSpec((B, 1, cap, 1), lambda e: (0, e, 0, 0)),
                  pl.BlockSpec(memory_space=pl.ANY),
                  pl.BlockSpec((1, D, F), lambda e: (e, 0, 0)),
                  pl.BlockSpec((1, D, F), lambda e: (e, 0, 0)),
                  pl.BlockSpec((1, F, D), lambda e: (e, 0, 0))],
        out_specs=pl.BlockSpec(memory_space=pl.ANY),
        out_shape=jax.ShapeDtypeStruct((T, D2), F32),
        scratch_shapes=[pltpu.SMEM((B * cap,), jnp.int32),
                        pltpu.VMEM((2, sub, D2), F32),
                        pltpu.SemaphoreType.DMA((5,))],
        input_output_aliases={2: 0},
        compiler_params=_params("arbitrary"),
        name="expert_ffn",
    )(idx.reshape(B * E * cap), gate, hx, w1, w3, w2)


def _rope_tables(positions, dim, reps):
    inv = ROPE_THETA ** (-jnp.arange(0, dim, 2, dtype=F32) / dim)
    ang = positions.astype(F32).reshape(-1, 1) * inv
    cos, sin = jnp.cos(ang), jnp.sin(ang)
    return (jnp.tile(jnp.concatenate([cos, cos], axis=-1), (1, reps)),
            jnp.tile(jnp.concatenate([-sin, sin], axis=-1), (1, reps)))


def _by_class(a, d):
    B, S = a.shape[:2]
    return jnp.swapaxes(a.reshape(B, S // d, d, *a.shape[2:]), 1, 2)


def kernel(x, mem, positions, norm_mix, w_in, da_lambda, da_subln, conv_w, w_branch, w_gate, b_gate,
           w_out, norm_cross, norm_mem, wq_x, wkv_x, wo_x, norm_moe, w_router, w_e1, w_e3, w_e2,
           norm_final):
    B, S, D = x.shape
    depth = w_in.shape[0]
    E = w_router.shape[-1]
    cap = CAPACITY_FACTOR * S // E
    T = B * S

    ca, sa = _rope_tables(positions, DA_QK_DIM, LANES // DA_QK_DIM)
    tabs_b = [_rope_tables(_by_class(positions, d), DIL_HEAD_DIM, 1) for d in DILATIONS]
    kv_all = memory_kv(mem, norm_mem, wkv_x.astype(BF16))

    hx = x
    for l in range(depth):
        lam_init = jnp.full((1, 1), 0.8 - 0.6 * math.exp(-0.3 * l), F32)
        w_in_l = w_in[l].astype(BF16)
        xn, *xn_cls = rmsnorm_with_classes(hx, norm_mix[l])
        xn = xn.reshape(T, D)
        proj3 = in_projection(xn, w_in_l, ca, sa, *tabs_b[0]).reshape(B, S, NAT_COLS)
        ya = diff_attention(proj3, da_lambda[l], lam_init, da_subln[l]).reshape(T, A_WIDTH)
        outs, lses = [], []
        for g, (window, d) in enumerate(DIL_PAIRS):
            if d == 1:
                qkv, first = proj3.reshape(B, 1, S, NAT_COLS), B_FIRST_TILE
            else:
                xg = xn_cls[[dd for dd in DILATIONS if dd > 1].index(d)].reshape(T, D)
                qkv = in_projection(xg, w_in_l, ca, sa, *tabs_b[g], group=g)
                qkv, first = qkv.reshape(B, d, S // d, B_TILES_PER_GROUP * COL_TILE), 0
            o, lse = dilated_attention(qkv, first, window, d, g)
            outs.append(o)
            lses.append(lse)
        yb = merge_groups(outs, lses).reshape(T, B_WIDTH)
        yc = short_conv_branch(proj3, conv_w[l]).reshape(T, C_WIDTH)
        merged = gate_merge(xn, ya, yb, yc, w_gate[l].astype(BF16), b_gate[l].reshape(1, -1),
                            w_branch[l].astype(BF16))
        h = matmul_residual(merged, w_out[l].astype(BF16), hx.reshape(T, -1))
        hx, aff_t = cross_attention_router(
            h.reshape(B, S, D), norm_cross[l], wq_x[l].astype(BF16), kv_all[l],
            wo_x[l].astype(BF16), norm_moe[l], w_router[l].T)
        idx, gate = expert_choice_select(aff_t, cap)
        hx = expert_ffn(idx, gate, hx.reshape(T, -1), w_e1[l].astype(BF16),
                        w_e3[l].astype(BF16), w_e2[l].astype(BF16), S).reshape(B, S, -1)
    return rmsnorm(hx.reshape(T, -1), norm_final, F32).reshape(B, S, D)
```

```python
import functools
import math

import jax
import jax.numpy as jnp
from jax import lax
from jax.experimental import pallas as pl
from jax.experimental.pallas import tpu as pltpu

F32 = jnp.float32
BF16 = jnp.bfloat16

EPS = 1e-6
ROPE_THETA = 10000.0
DA_HEADS = 4
DA_QK_DIM = 64
DA_V_DIM = 2 * DA_QK_DIM
DIL_PAIRS = ((128, 1), (512, 4), (2048, 16))
DIL_HEADS = 4
DIL_HEAD_DIM = 128
CONV_CH = 1024
A_WIDTH = DA_HEADS * DA_V_DIM
B_WIDTH = DIL_HEADS * DIL_HEAD_DIM
C_WIDTH = CONV_CH
MIX_WIDTH = A_WIDTH + B_WIDTH + C_WIDTH
N_BRANCH = 3
IN_COLS = 3 * A_WIDTH + len(DIL_PAIRS) * 3 * B_WIDTH + 3 * CONV_CH
X_HEADS = 4
X_HEAD_DIM = 128
CAPACITY_FACTOR = 2

LANES = 128
V7X_VMEM_BYTES = 64 * 1024 * 1024
VMEM_LIMIT = V7X_VMEM_BYTES - 8 * 1024 * 1024
COL_TILE = 512
BF16_ROWS = 16
DMA_PRIORITIES = 2
DIL_QBLK = 256
NEG_BIG = -1e30


def _params(*sem):
    return pltpu.CompilerParams(dimension_semantics=sem, vmem_limit_bytes=VMEM_LIMIT)


def _rms(x, g):
    return x * lax.rsqrt(jnp.mean(x * x, axis=-1, keepdims=True) + EPS) * g


def _dot(a, b):
    return jnp.dot(a, b, preferred_element_type=F32)


def _dot_nt(a, b, **kw):
    return lax.dot_general(a, b, (((1,), (1,)), ((), ())), preferred_element_type=F32, **kw)


def _rms_kernel(x_ref, g_ref, o_ref):
    o_ref[...] = _rms(x_ref[...], g_ref[...]).astype(o_ref.dtype)


def rmsnorm(x, g, out_dtype):
    T = x.shape[0]
    D = g.shape[0]
    tm = min(512, T)
    return pl.pallas_call(
        _rms_kernel,
        grid=(T // tm,),
        in_specs=[pl.BlockSpec((tm, D), lambda i: (i, 0)),
                  pl.BlockSpec((1, D), lambda i: (0, 0))],
        out_specs=pl.BlockSpec((tm, D), lambda i: (i, 0)),
        out_shape=jax.ShapeDtypeStruct((T, D), out_dtype),
        compiler_params=_params("parallel"),
        name="rmsnorm",
    )(x, g.reshape(1, D))


DILATIONS = tuple(d for _, d in DIL_PAIRS)


def _rms_classes_kernel(x_ref, g_ref, o_ref, *rest):
    class_refs, scr = rest[:-1], rest[-1]
    y = _rms(x_ref[0], g_ref[...])
    o_ref[0] = y.astype(o_ref.dtype)
    tm, D = y.shape
    for c in range(D // LANES):
        cs = slice(c * LANES, (c + 1) * LANES)
        scr[c] = y[:, cs]
        for ref, d in zip(class_refs, [d for d in DILATIONS if d > 1]):
            for r in range(d):
                ref[0, r, :, cs] = scr[c, pl.ds(r, tm // d, stride=d), :].astype(ref.dtype)


def rmsnorm_with_classes(x3, g):
    B, S, _ = x3.shape
    D = g.shape[0]
    tm = min(512, S)
    ds = [d for d in DILATIONS if d > 1]
    return pl.pallas_call(
        _rms_classes_kernel,
        grid=(B, S // tm),
        in_specs=[pl.BlockSpec((1, tm, D), lambda b, i: (b, i, 0)),
                  pl.BlockSpec((1, D), lambda b, i: (0, 0))],
        out_specs=[pl.BlockSpec((1, tm, D), lambda b, i: (b, i, 0))]
        + [pl.BlockSpec((1, d, tm // d, D), lambda b, i: (b, 0, i, 0)) for d in ds],
        out_shape=[jax.ShapeDtypeStruct((B, S, D), BF16)]
        + [jax.ShapeDtypeStruct((B, d, S // d, D), BF16) for d in ds],
        scratch_shapes=[pltpu.VMEM((D // LANES, tm, LANES), F32)],
        compiler_params=_params("parallel", "parallel"),
        name="rmsnorm_classes",
    )(x3, g.reshape(1, D))


N_COL_TILES = IN_COLS // COL_TILE
A_ROPE_TILES = 2 * A_WIDTH // COL_TILE
B_FIRST_TILE = 3 * A_WIDTH // COL_TILE
B_TILES_PER_GROUP = 3 * B_WIDTH // COL_TILE
B_ROPE_PER_GROUP = 2 * B_WIDTH // COL_TILE
C_FIRST_TILE = B_FIRST_TILE + len(DIL_PAIRS) * B_TILES_PER_GROUP


def _inproj_kernel(x_ref, w_ref, ca_ref, sa_ref, cb_ref, sb_ref, o_ref, *, col_map):
    j = col_map(pl.program_id(1))
    acc = _dot(x_ref[...], w_ref[...])
    jb = j - B_FIRST_TILE
    is_a = j < A_ROPE_TILES
    is_b = (j >= B_FIRST_TILE) & (j < C_FIRST_TILE) & (jb % B_TILES_PER_GROUP < B_ROPE_PER_GROUP)

    @pl.when(is_a)
    def _():
        c, s = ca_ref[...], sa_ref[...]
        lane = lax.broadcasted_iota(jnp.int32, c.shape, 1)
        first = (lane % DA_QK_DIM) < (DA_QK_DIM // 2)
        for t in range(COL_TILE // LANES):
            x = acc[:, t * LANES:(t + 1) * LANES]
            rot = jnp.where(first, pltpu.roll(x, LANES - DA_QK_DIM // 2, 1),
                            pltpu.roll(x, DA_QK_DIM // 2, 1))
            o_ref[:, t * LANES:(t + 1) * LANES] = (x * c + rot * s).astype(o_ref.dtype)

    @pl.when(is_b)
    def _():
        c, s = cb_ref[...], sb_ref[...]
        for t in range(COL_TILE // LANES):
            x = acc[:, t * LANES:(t + 1) * LANES]
            rot = pltpu.roll(x, DIL_HEAD_DIM // 2, 1)
            o_ref[:, t * LANES:(t + 1) * LANES] = (x * c + rot * s).astype(o_ref.dtype)

    @pl.when(jnp.logical_not(is_a | is_b))
    def _():
        o_ref[...] = acc.astype(o_ref.dtype)


NAT_TILES = N_COL_TILES - (len(DIL_PAIRS) - 1) * B_TILES_PER_GROUP
NAT_COLS = NAT_TILES * COL_TILE
NAT_C_FIRST_TILE = B_FIRST_TILE + B_TILES_PER_GROUP


def _natural_col_map(j):
    return j + jnp.where(j >= NAT_C_FIRST_TILE, (len(DIL_PAIRS) - 1) * B_TILES_PER_GROUP, 0)


def in_projection(xn, w, ca, sa, cb, sb, group=None):
    T, D = xn.shape
    tm = min(2048, T)
    if group is None:
        n_tiles, col_map = NAT_TILES, _natural_col_map
    else:
        first = B_FIRST_TILE + group * B_TILES_PER_GROUP
        n_tiles, col_map = B_TILES_PER_GROUP, lambda j: j + first
    tab = pl.BlockSpec((tm, LANES), lambda i, j: (i, 0))
    return pl.pallas_call(
        functools.partial(_inproj_kernel, col_map=col_map),
        grid=(T // tm, n_tiles),
        in_specs=[pl.BlockSpec((tm, D), lambda i, j: (i, 0)),
                  pl.BlockSpec((D, COL_TILE), lambda i, j: (0, col_map(j))),
                  tab, tab, tab, tab],
        out_specs=pl.BlockSpec((tm, COL_TILE), lambda i, j: (i, j)),
        out_shape=jax.ShapeDtypeStruct((T, n_tiles * COL_TILE), BF16),
        compiler_params=_params("parallel", "arbitrary"),
        name="in_projection" if group is None else f"in_projection_g{group}",
    )(xn, w, ca, sa, cb, sb)


KV_UNROLL = 8


def _diff_attn_kernel(q_ref, k_ref, v_ref, lam_ref, li_ref, g_ref, o_ref, vext_ref, *, tk):
    S = k_ref.shape[1]
    tq = q_ref.shape[1]

    @pl.when(pl.program_id(2) == 0)
    def _():
        vext_ref[:, :DA_V_DIM] = v_ref[0]
        vext_ref[:, DA_V_DIM:] = jnp.ones((S, DA_V_DIM), BF16)

    q = q_ref[0] * jnp.asarray(DA_QK_DIM ** -0.5, BF16)
    lane = lax.broadcasted_iota(jnp.int32, q.shape, 1)
    zero = jnp.zeros_like(q)
    q0 = jnp.where(lane < DA_QK_DIM, q, zero)
    q1 = jnp.where(lane >= DA_QK_DIM, q, zero)

    def step(qm, kj, vj, m, a):
        s = _dot_nt(qm, kj)
        m_new = jnp.maximum(m, jnp.max(s, axis=-1, keepdims=True))
        p = jnp.exp((s - m_new).astype(BF16))
        a = jnp.exp(m - m_new) * a + _dot(p, vj)
        return m_new, a

    def body(j, carry):
        m0, a0, m1, a1 = carry
        off = pl.multiple_of(j * tk, tk)
        kj = k_ref[0, pl.ds(off, tk), :]
        vj = vext_ref[pl.ds(off, tk), :]
        m0, a0 = step(q0, kj, vj, m0, a0)
        m1, a1 = step(q1, kj, vj, m1, a1)
        return m0, a0, m1, a1

    mi = jnp.full((tq, 1), NEG_BIG, F32)
    ai = jnp.zeros((tq, 2 * DA_V_DIM), F32)
    _, a0, _, a1 = lax.fori_loop(0, S // tk, body, (mi, ai, mi, ai), unroll=KV_UNROLL)

    lp = lam_ref[...]
    lam_init = li_ref[...]
    lam = (jnp.exp(jnp.sum(lp[0:1] * lp[1:2], axis=-1, keepdims=True))
           - jnp.exp(jnp.sum(lp[2:3] * lp[3:4], axis=-1, keepdims=True)) + lam_init)
    o = a0[:, :DA_V_DIM] / a0[:, DA_V_DIM:] - lam * (a1[:, :DA_V_DIM] / a1[:, DA_V_DIM:])
    o_ref[0] = (_rms(o, g_ref[...]) * (1.0 - lam_init)).astype(o_ref.dtype)


def diff_attention(proj3, lam_p, lam_init, sub_gain):
    B, S, _ = proj3.shape
    tq = min(1024, S)
    tk = min(512, S)
    assert (S // tk) % KV_UNROLL == 0
    kb = A_WIDTH // DA_V_DIM
    return pl.pallas_call(
        functools.partial(_diff_attn_kernel, tk=tk),
        grid=(B, DA_HEADS, S // tq),
        in_specs=[pl.BlockSpec((1, tq, DA_V_DIM), lambda b, h, i: (b, i, h)),
                  pl.BlockSpec((1, S, DA_V_DIM), lambda b, h, i: (b, 0, kb + h)),
                  pl.BlockSpec((1, S, DA_V_DIM), lambda b, h, i: (b, 0, 2 * kb + h)),
                  pl.BlockSpec((4, DA_QK_DIM), lambda b, h, i: (0, 0)),
                  pl.BlockSpec((1, 1), lambda b, h, i: (0, 0)),
                  pl.BlockSpec((1, DA_V_DIM), lambda b, h, i: (0, 0))],
        out_specs=pl.BlockSpec((1, tq, DA_V_DIM), lambda b, h, i: (b, i, h)),
        out_shape=jax.ShapeDtypeStruct((B, S, A_WIDTH), BF16),
        scratch_shapes=[pltpu.VMEM((S, 2 * DA_V_DIM), BF16)],
        compiler_params=_params("parallel", "parallel", "arbitrary"),
        name="diff_attention",
    )(proj3, proj3, proj3, lam_p, lam_init, sub_gain.reshape(1, DA_V_DIM))


def _dilated_kernel(q_ref, k_ref, v_ref, o_ref, lse_ref, *, n_side):
    i = pl.program_id(2)
    tq = q_ref.shape[2]
    L = k_ref.shape[2]
    W = tq + 2 * n_side
    start = pl.multiple_of(jnp.clip(i * tq - n_side, 0, L - W), n_side)
    qpos = i * tq + lax.broadcasted_iota(jnp.int32, (tq, W), 0)
    kpos = start + lax.broadcasted_iota(jnp.int32, (tq, W), 1)
    valid = jnp.abs(kpos - qpos) <= n_side
    scale = DIL_HEAD_DIM ** -0.5
    for h in range(DIL_HEADS):
        hs = slice(h * DIL_HEAD_DIM, (h + 1) * DIL_HEAD_DIM)
        kw = k_ref[0, 0, pl.ds(start, W), hs]
        vw = v_ref[0, 0, pl.ds(start, W), hs]
        s = jnp.where(valid, _dot_nt(q_ref[0, 0, :, hs], kw) * scale, NEG_BIG)
        m = jnp.max(s, axis=-1, keepdims=True)
        p = jnp.exp((s - m).astype(BF16))
        ol = _dot(p, jnp.concatenate([vw, jnp.ones_like(vw)], axis=1))
        l = ol[:, DIL_HEAD_DIM:]
        o_ref[0, 0, :, hs] = ol[:, :DIL_HEAD_DIM] / l
        lse_ref[0, 0, :, hs] = m + jnp.log(l)


def dilated_attention(qkv, first_tile, window, dilation, group):
    B, d, L, _ = qkv.shape
    n_side = (window // 2) // dilation
    tq = next(t for t in (DIL_QBLK, DIL_QBLK // 2) if t + 2 * n_side <= L and L % t == 0)
    assert d == dilation and n_side % BF16_ROWS == 0

    def resident(kind):
        return pl.BlockSpec((1, 1, L, B_WIDTH), lambda b, r, i: (b, r, 0, first_tile + kind))

    out_spec = pl.BlockSpec((1, 1, tq, B_WIDTH), lambda b, r, i: (b, r, i, 0))
    return pl.pallas_call(
        functools.partial(_dilated_kernel, n_side=n_side),
        grid=(B, d, L // tq),
        in_specs=[pl.BlockSpec((1, 1, tq, B_WIDTH), lambda b, r, i: (b, r, i, first_tile)),
                  resident(1), resident(2)],
        out_specs=[out_spec, out_spec],
        out_shape=[jax.ShapeDtypeStruct((B, d, L, B_WIDTH), F32)] * 2,
        compiler_params=_params("parallel", "parallel", "arbitrary"),
        name=f"dilated_attention_g{group}",
    )(qkv, qkv, qkv)


def _merge_kernel(*refs):
    n = len(DILATIONS)
    ins, y_ref, scr = refs[:2 * n], refs[2 * n], refs[2 * n + 1:]
    tm = y_ref.shape[1]
    for c in range(B_WIDTH // LANES):
        cs = slice(c * LANES, (c + 1) * LANES)
        vals, k = [], 0
        for g, d in enumerate(DILATIONS):
            o_ref, l_ref = ins[2 * g], ins[2 * g + 1]
            if d == 1:
                vals.append((o_ref[0, 0, :, cs], l_ref[0, 0, :, cs]))
                continue
            so, sl = scr[2 * k], scr[2 * k + 1]
            k += 1
            for r in range(d):
                so[c, pl.ds(r, tm // d, stride=d), :] = o_ref[0, r, :, cs]
                sl[c, pl.ds(r, tm // d, stride=d), :] = l_ref[0, r, :, cs]
            vals.append((so[c], sl[c]))
        m = functools.reduce(jnp.maximum, [l for _, l in vals])
        ws = [jnp.exp(l - m) for _, l in vals]
        num = sum(w * o for w, (o, _) in zip(ws, vals))
        y_ref[0, :, cs] = (num / sum(ws)).astype(y_ref.dtype)


def merge_groups(outs, lses):
    B, _, S, _ = outs[0].shape
    tm = min(512, S)
    specs, args = [], []
    for d, o, l in zip(DILATIONS, outs, lses):
        sp = pl.BlockSpec((1, d, tm // d, B_WIDTH), lambda b, i: (b, 0, i, 0))
        specs += [sp, sp]
        args += [o, l]
    n_scr = 2 * sum(1 for d in DILATIONS if d > 1)
    return pl.pallas_call(
        _merge_kernel,
        grid=(B, S // tm),
        in_specs=specs,
        out_specs=pl.BlockSpec((1, tm, B_WIDTH), lambda b, i: (b, i, 0)),
        out_shape=jax.ShapeDtypeStruct((B, S, B_WIDTH), BF16),
        scratch_shapes=[pltpu.VMEM((B_WIDTH // LANES, tm, LANES), F32)] * n_scr,
        compiler_params=_params("parallel", "parallel"),
        name="merge_groups",
    )(*args)


HALO_ROWS = 16


def _conv_kernel(h_ref, gb_ref, gc_ref, hp_ref, gp_ref, hn_ref, gn_ref, w_ref, o_ref):
    i = pl.program_id(1)
    ts = h_ref.shape[1]
    u = gc_ref[0].astype(F32) * h_ref[0].astype(F32)
    up = (gp_ref[0].astype(F32) * hp_ref[0].astype(F32))[HALO_ROWS - 1:HALO_ROWS]
    un = (gn_ref[0].astype(F32) * hn_ref[0].astype(F32))[0:1]
    up = jnp.where(i > 0, up, 0.0)
    un = jnp.where(i < pl.num_programs(1) - 1, un, 0.0)
    rows = lax.broadcasted_iota(jnp.int32, u.shape, 0)
    u_prev = jnp.where(rows == 0, up, pltpu.roll(u, 1, 0))
    u_next = jnp.where(rows == ts - 1, un, pltpu.roll(u, ts - 1, 0))
    w = w_ref[...]
    y = gb_ref[0].astype(F32) * (w[0:1] * u_prev + w[1:2] * u + w[2:3] * u_next)
    o_ref[0] = y.astype(o_ref.dtype)


def short_conv_branch(proj3, conv_w):
    B, S, cols = proj3.shape
    ts = min(512, S)
    c0 = (cols - 3 * CONV_CH) // CONV_CH
    hb = ts // HALO_ROWS
    n_halo = S // HALO_ROWS

    def main(c):
        return pl.BlockSpec((1, ts, CONV_CH), lambda b, i: (b, i, c))

    def prev(c):
        return pl.BlockSpec((1, HALO_ROWS, CONV_CH), lambda b, i: (b, jnp.maximum(i * hb - 1, 0), c))

    def nxt(c):
        return pl.BlockSpec((1, HALO_ROWS, CONV_CH),
                            lambda b, i: (b, jnp.minimum((i + 1) * hb, n_halo - 1), c))

    return pl.pallas_call(
        _conv_kernel,
        grid=(B, S // ts),
        in_specs=[main(c0), main(c0 + 1), main(c0 + 2), prev(c0), prev(c0 + 2), nxt(c0), nxt(c0 + 2),
                  pl.BlockSpec((3, CONV_CH), lambda b, i: (0, 0))],
        out_specs=pl.BlockSpec((1, ts, CONV_CH), lambda b, i: (b, i, 0)),
        out_shape=jax.ShapeDtypeStruct((B, S, CONV_CH), BF16),
        compiler_params=_params("parallel", "parallel"),
        name="short_conv",
    )(proj3, proj3, proj3, proj3, proj3, proj3, proj3, conv_w)


def _gate_merge_kernel(xn_ref, ya_ref, yb_ref, yc_ref, wg0_ref, wg1_ref, wg2_ref,
                       b0_ref, b1_ref, b2_ref, wa_ref, wb_ref, wc_ref, o_ref):
    xn = xn_ref[...]
    acc = jax.nn.sigmoid(_dot(xn, wg0_ref[...]) + b0_ref[...]) * _dot(ya_ref[...], wa_ref[...])
    acc += jax.nn.sigmoid(_dot(xn, wg1_ref[...]) + b1_ref[...]) * _dot(yb_ref[...], wb_ref[...])
    acc += jax.nn.sigmoid(_dot(xn, wg2_ref[...]) + b2_ref[...]) * _dot(yc_ref[...], wc_ref[...])
    o_ref[...] = acc.astype(o_ref.dtype)


def gate_merge(xn, ya, yb, yc, w_gate, b_gate, w_br):
    T, D = xn.shape
    tm = min(1024, T)
    tn = min(512, D)
    nj = D // tn
    assert A_WIDTH == B_WIDTH and C_WIDTH == 2 * A_WIDTH

    def wg(k):
        return pl.BlockSpec((D, tn), lambda i, j: (0, k * nj + j))

    def bg(k):
        return pl.BlockSpec((1, tn), lambda i, j: (0, k * nj + j))

    def rowblk(w):
        return pl.BlockSpec((tm, w), lambda i, j: (i, 0))

    return pl.pallas_call(
        _gate_merge_kernel,
        grid=(T // tm, nj),
        in_specs=[rowblk(D), rowblk(A_WIDTH), rowblk(B_WIDTH), rowblk(C_WIDTH),
                  wg(0), wg(1), wg(2), bg(0), bg(1), bg(2),
                  pl.BlockSpec((A_WIDTH, tn), lambda i, j: (0, j)),
                  pl.BlockSpec((B_WIDTH, tn), lambda i, j: (1, j)),
                  pl.BlockSpec((C_WIDTH, tn), lambda i, j: (1, j))],
        out_specs=pl.BlockSpec((tm, tn), lambda i, j: (i, j)),
        out_shape=jax.ShapeDtypeStruct((T, D), BF16),
        compiler_params=_params("parallel", "arbitrary"),
        name="gate_merge",
    )(xn, ya, yb, yc, w_gate, w_gate, w_gate, b_gate, b_gate, b_gate, w_br, w_br, w_br)


def _matmul_residual_kernel(x_ref, w_ref, r_ref, o_ref):
    o_ref[...] = r_ref[...] + _dot(x_ref[...], w_ref[...])


def matmul_residual(x, w, res):
    T, K = x.shape
    N = w.shape[1]
    tm = min(1024, T)
    tn = min(1024, N)
    return pl.pallas_call(
        _matmul_residual_kernel,
        grid=(T // tm, N // tn),
        in_specs=[pl.BlockSpec((tm, K), lambda i, j: (i, 0)),
                  pl.BlockSpec((K, tn), lambda i, j: (0, j)),
                  pl.BlockSpec((tm, tn), lambda i, j: (i, j))],
        out_specs=pl.BlockSpec((tm, tn), lambda i, j: (i, j)),
        out_shape=jax.ShapeDtypeStruct((T, N), F32),
        compiler_params=_params("parallel", "arbitrary"),
        name="matmul_residual",
    )(x, w, res)


def _memkv_kernel(mem_ref, g_ref, w_ref, o_ref):
    o_ref[0, 0] = _dot(_rms(mem_ref[0], g_ref[0]).astype(BF16), w_ref[0]).astype(o_ref.dtype)


def memory_kv(mem, norm_mem, wkv):
    B, M, D = mem.shape
    L, _, N = wkv.shape
    return pl.pallas_call(
        _memkv_kernel,
        grid=(L, B),
        in_specs=[pl.BlockSpec((1, M, D), lambda l, b: (b, 0, 0)),
                  pl.BlockSpec((1, 1, D), lambda l, b: (l, 0, 0)),
                  pl.BlockSpec((1, D, N), lambda l, b: (l, 0, 0))],
        out_specs=pl.BlockSpec((1, 1, M, N), lambda l, b: (l, b, 0, 0)),
        out_shape=jax.ShapeDtypeStruct((L, B, M, N), BF16),
        compiler_params=_params("parallel", "parallel"),
        name="memory_kv",
    )(mem, norm_mem.reshape(L, 1, D), wkv)


def _pack_bf16_pairs(x):
    m = x.shape[1] // 2
    hi = pltpu.bitcast(x[:, :m].astype(BF16).astype(F32), jnp.uint32)
    lo = pltpu.bitcast(x[:, m:].astype(BF16).astype(F32), jnp.uint32)
    return pltpu.bitcast(hi | (lo >> 16), F32)


def _unpack_bf16_pairs(w):
    bits = pltpu.bitcast(w, jnp.uint32)
    hi = pltpu.bitcast(bits & jnp.uint32(0xFFFF0000), F32).astype(BF16)
    lo = pltpu.bitcast(bits << 16, F32).astype(BF16)
    return jnp.concatenate([hi, lo], axis=1)


def _cross_kernel(h_ref, gx_ref, wq_ref, kv_ref, wo_ref, gm_ref, wr_ref, hx_ref, aff_ref):
    h = h_ref[0]
    D = h.shape[1]
    q = _dot(_rms(h, gx_ref[...]).astype(BF16), wq_ref[...]) * (X_HEAD_DIM ** -0.5)
    q = q.astype(BF16)
    kv = kv_ref[0]
    outs = []
    for hd in range(X_HEADS):
        ks = slice(hd * X_HEAD_DIM, (hd + 1) * X_HEAD_DIM)
        vs = slice((X_HEADS + hd) * X_HEAD_DIM, (X_HEADS + hd + 1) * X_HEAD_DIM)
        s = _dot_nt(q[:, ks], kv[:, ks])
        e = jnp.exp(s - jnp.max(s, axis=-1, keepdims=True))
        p = e / jnp.sum(e, axis=-1, keepdims=True)
        outs.append(_dot(p.astype(BF16), kv[:, vs]))
    o = jnp.concatenate(outs, axis=1).astype(BF16)
    hn = h + _dot(o, wo_ref[...])
    xn = _rms(hn, gm_ref[...])
    hx_ref[0, :, :D] = hn
    hx_ref[0, :, D:] = _pack_bf16_pairs(xn)
    lt = _dot_nt(wr_ref[...], xn, precision=lax.Precision.HIGHEST)
    e = jnp.exp(lt - jnp.max(lt, axis=0, keepdims=True))
    aff_ref[0] = e / jnp.sum(e, axis=0, keepdims=True)


def cross_attention_router(h3, g_cross, wq, kv, wo, g_moe, w_router_t):
    B, S, D = h3.shape
    M, NKV = kv.shape[1:]
    E = w_router_t.shape[0]
    NQ = wq.shape[1]
    tm = min(512, S)

    def full(shape):
        return pl.BlockSpec(shape, lambda b, i: (0,) * len(shape))

    return pl.pallas_call(
        _cross_kernel,
        grid=(B, S // tm),
        in_specs=[pl.BlockSpec((1, tm, D), lambda b, i: (b, i, 0)),
                  full((1, D)), full((D, NQ)),
                  pl.BlockSpec((1, M, NKV), lambda b, i: (b, 0, 0)),
                  full((NQ, D)), full((1, D)), full((E, D))],
        out_specs=[pl.BlockSpec((1, tm, D + D // 2), lambda b, i: (b, i, 0)),
                   pl.BlockSpec((1, E, tm), lambda b, i: (b, 0, i))],
        out_shape=[jax.ShapeDtypeStruct((B, S, D + D // 2), F32),
                   jax.ShapeDtypeStruct((B, E, S), F32)],
        compiler_params=_params("parallel", "parallel"),
        name="cross_attention_router",
    )(h3, g_cross.reshape(1, D), wq, kv, wo, g_moe.reshape(1, D), w_router_t)


SLOT_BLK = 128


def _select_kernel(aff_ref, cs_ref, w_ref, ts_ref, first_ref, *, cap):
    E, C, _ = aff_ref.shape[1:]
    R = E * C
    a = aff_ref[0].reshape(R, LANES)
    bits = pltpu.bitcast(a, jnp.int32)
    rid = lax.broadcasted_iota(jnp.int32, (R, LANES), 0)

    ii = lax.broadcasted_iota(jnp.int32, (LANES, LANES), 0)
    jj = lax.broadcasted_iota(jnp.int32, (LANES, LANES), 1)
    upper = jnp.where(ii <= jj, 1.0, 0.0).astype(BF16)
    ones = jnp.ones((LANES, LANES), BF16)
    ri = lax.broadcasted_iota(jnp.int32, (R, R), 0) // C
    rj = lax.broadcasted_iota(jnp.int32, (R, R), 1)
    same_expert = jnp.where(rj // C == ri, 1.0, 0.0).astype(BF16)
    rows_before = jnp.where((rj // C == ri) & (rj < lax.broadcasted_iota(jnp.int32, (R, R), 0)),
                            1.0, 0.0).astype(BF16)

    def row_totals(x):
        return _dot(x.astype(BF16), ones).astype(BF16)

    def count(x):
        return _dot(same_expert, row_totals(x))

    def prefix_count(x):
        return _dot(x.astype(BF16), upper) + _dot(rows_before, row_totals(x))

    thr = jnp.zeros((R, LANES), jnp.int32)
    for bit in range(30, -1, -1):
        cand = thr | jnp.int32(1 << bit)
        thr = jnp.where(count(jnp.where(bits >= cand, 1.0, 0.0)) >= cap, cand, thr)
    gt_f = jnp.where(bits > thr, 1.0, 0.0)
    eq_f = jnp.where(bits == thr, 1.0, 0.0)

    need = cap - count(gt_f)
    eq_rank = prefix_count(eq_f) - eq_f
    sel = gt_f + eq_f * jnp.where(eq_rank < need, 1.0, 0.0)
    csum = prefix_count(sel)
    tok = ((rid % C) * LANES + lax.broadcasted_iota(jnp.int32, (R, LANES), 1)).astype(F32)
    cs_ref[0] = csum
    w_ref[0] = a * sel
    ts_ref[0] = tok * sel
    before = _dot(rows_before, row_totals(sel))
    for blk in range(R // LANES):
        d = jnp.where(ii == jj, before[blk * LANES:(blk + 1) * LANES], 0.0)
        first_ref[0, blk:blk + 1, :] = jnp.sum(d, axis=0, keepdims=True).astype(jnp.int32)


def _resolve_kernel(first_ref, cs_ref, w_ref, ts_ref, idx_ref, gate_ref, acc_i, acc_g, *, cap):
    b = pl.program_id(0)
    e = pl.program_id(1)
    C = cs_ref.shape[1]
    n_blk = cap // SLOT_BLK
    acc_i[...] = jnp.zeros_like(acc_i)
    acc_g[...] = jnp.zeros_like(acc_g)

    def chunk(c, _):
        p0 = first_ref[b, e * C + c]
        p1 = jnp.where(c + 1 < C, first_ref[b, e * C + jnp.minimum(c + 1, C - 1)], cap)
        cs_row = cs_ref[0, pl.ds(c, 1), :]
        ts_row = ts_ref[0, pl.ds(c, 1), :]
        w_row = w_ref[0, pl.ds(c, 1), :]
        for j in range(LANES // SLOT_BLK + 1):
            blk = p0 // SLOT_BLK + j

            @pl.when((blk * SLOT_BLK < p1) & (blk < n_blk))
            def _():
                off = pl.multiple_of(blk * SLOT_BLK, SLOT_BLK)
                k1 = (lax.broadcasted_iota(jnp.int32, (SLOT_BLK, LANES), 0) + (off + 1)).astype(F32)
                hit = cs_row == k1
                rows = pl.ds(off, SLOT_BLK)
                acc_i[rows, :] = acc_i[rows, :] + jnp.where(hit, ts_row, 0.0)
                acc_g[rows, :] = acc_g[rows, :] + jnp.where(hit, w_row, 0.0)
        return 0

    lax.fori_loop(0, C, chunk, 0)
    idx_ref[0, 0] = jnp.sum(acc_i[...], axis=1, keepdims=True).astype(jnp.int32)
    gate_ref[0, 0] = jnp.sum(acc_g[...], axis=1, keepdims=True)


def expert_choice_select(aff_t, cap):
    B, E, S = aff_t.shape
    C = S // LANES
    R = E * C
    assert cap % SLOT_BLK == 0 and R % LANES == 0
    rows = pl.BlockSpec((1, R, LANES), lambda b: (b, 0, 0))
    cs, w, ts, first = pl.pallas_call(
        functools.partial(_select_kernel, cap=cap),
        grid=(B,),
        in_specs=[pl.BlockSpec((1, E, C, LANES), lambda b: (b, 0, 0, 0))],
        out_specs=[rows, rows, rows, pl.BlockSpec((1, R // LANES, LANES), lambda b: (b, 0, 0))],
        out_shape=[jax.ShapeDtypeStruct((B, R, LANES), F32)] * 3
        + [jax.ShapeDtypeStruct((B, R // LANES, LANES), jnp.int32)],
        compiler_params=_params("parallel"),
        name="expert_choice_select",
    )(aff_t.reshape(B, E, C, LANES))

    chunk_rows = pl.BlockSpec((1, C, LANES), lambda b, e, first: (b * E + e, 0, 0))
    slots = pl.BlockSpec((1, 1, cap, 1), lambda b, e, first: (b, e, 0, 0))
    return pl.pallas_call(
        functools.partial(_resolve_kernel, cap=cap),
        grid_spec=pltpu.PrefetchScalarGridSpec(
            num_scalar_prefetch=1,
            grid=(B, E),
            in_specs=[chunk_rows] * 3,
            out_specs=[slots, slots],
            scratch_shapes=[pltpu.VMEM((cap, LANES), F32)] * 2),
        out_shape=[jax.ShapeDtypeStruct((B, E, cap, 1), jnp.int32),
                   jax.ShapeDtypeStruct((B, E, cap, 1), F32)],
        compiler_params=_params("parallel", "parallel"),
        name="expert_choice_resolve",
    )(first.reshape(B, R), *[x.reshape(B * E, C, LANES) for x in (cs, w, ts)])


def _expert_ffn_kernel(idx_hbm, gate_ref, hx_in_hbm, w1_ref, w3_ref, w2_ref, hx_hbm,
                       idx_smem, buf, sems, *, sub, seq_len):
    del hx_in_hbm
    e = pl.program_id(0)
    n_exp = pl.num_programs(0)
    n_seq = gate_ref.shape[0]
    cap = idx_smem.shape[0] // n_seq
    D = w2_ref.shape[2]
    per_seq = cap // sub
    n_tiles = n_seq * per_seq
    for b in range(n_seq):
        cp = pltpu.make_async_copy(idx_hbm.at[pl.ds((b * n_exp + e) * cap, cap)],
                                   idx_smem.at[pl.ds(b * cap, cap)], sems.at[0])
        cp.start()
        cp.wait()

    def token_row(t, r):
        return pl.ds((t // per_seq) * seq_len + idx_smem[t * sub + r], 1)

    def gather(t, slot):
        for r in range(sub):
            pltpu.make_async_copy(hx_hbm.at[token_row(t, r), :], buf.at[slot, pl.ds(r, 1), :],
                                  sems.at[1 + slot]).start(priority=r % DMA_PRIORITIES)

    def wait_gather(slot):
        pltpu.make_async_copy(hx_hbm.at[pl.ds(0, sub), :], buf.at[slot], sems.at[1 + slot]).wait()

    def write_back(t, slot):
        for r in range(sub):
            pltpu.make_async_copy(buf.at[slot, pl.ds(r, 1), pl.ds(0, D)],
                                  hx_hbm.at[token_row(t, r), pl.ds(0, D)],
                                  sems.at[3 + slot]).start(priority=r % DMA_PRIORITIES)

    def wait_write_back(slot):
        pltpu.make_async_copy(buf.at[slot, :, pl.ds(0, D)], hx_hbm.at[pl.ds(0, sub), pl.ds(0, D)],
                              sems.at[3 + slot]).wait()

    def tile(t, slot, last):
        wait_gather(slot)
        if not last:
            @pl.when(t >= 1)
            def _():
                wait_write_back(1 - slot)
            gather(t + 1, 1 - slot)
        xg = _unpack_bf16_pairs(buf[slot, :, D:])
        a = _dot(xg, w1_ref[0])
        c = _dot(xg, w3_ref[0])
        hid = (a * jax.nn.sigmoid(a) * c).astype(BF16)
        y = _dot(hid, w2_ref[0])
        off = pl.multiple_of((t % per_seq) * sub, sub)
        g = gate_ref[t // per_seq, 0, pl.ds(off, sub), :]
        buf[slot, :, :D] = buf[slot, :, :D] + y * g
        write_back(t, slot)

    gather(0, 0)

    def pair(p, _):
        tile(2 * p, 0, False)
        tile(2 * p + 1, 1, False)
        return 0

    lax.fori_loop(0, n_tiles // 2 - 1, pair, 0)
    tile(n_tiles - 2, 0, False)
    tile(n_tiles - 1, 1, True)
    wait_write_back(0)
    wait_write_back(1)


def expert_ffn(idx, gate, hx, w1, w3, w2, seq_len):
    T, D2 = hx.shape
    B, E, cap, _ = gate.shape
    D, F = w1.shape[1:]
    sub = min(256, cap)
    assert D2 == D + D // 2 and (B * cap // sub) % 2 == 0 and cap % 1024 == 0
    return pl.pallas_call(
        functools.partial(_expert_ffn_kernel, sub=sub, seq_len=seq_len),
        grid=(E,),
        in_specs=[pl.BlockSpec(memory_space=pl.ANY),
                  pl.BlockSpec((B, 1, cap, 1), lambda e: (0, e, 0, 0)),
                  pl.BlockSpec(memory_space=pl.ANY),
                  pl.BlockSpec((1, D, F), lambda e: (e, 0, 0)),
                  pl.BlockSpec((1, D, F), lambda e: (e, 0, 0)),
                  pl.BlockSpec((1, F, D), lambda e: (e, 0, 0))],
        out_specs=pl.BlockSpec(memory_space=pl.ANY),
        out_shape=jax.ShapeDtypeStruct((T, D2), F32),
        scratch_shapes=[pltpu.SMEM((B * cap,), jnp.int32),
                        pltpu.VMEM((2, sub, D2), F32),
                        pltpu.SemaphoreType.DMA((5,))],
        input_output_aliases={2: 0},
        compiler_params=_params("arbitrary"),
        name="expert_ffn",
    )(idx.reshape(B * E * cap), gate, hx, w1, w3, w2)


def _rope_tables(positions, dim, reps):
    inv = ROPE_THETA ** (-jnp.arange(0, dim, 2, dtype=F32) / dim)
    ang = positions.astype(F32).reshape(-1, 1) * inv
    cos, sin = jnp.cos(ang), jnp.sin(ang)
    return (jnp.tile(jnp.concatenate([cos, cos], axis=-1), (1, reps)),
            jnp.tile(jnp.concatenate([-sin, sin], axis=-1), (1, reps)))


def _by_class(a, d):
    B, S = a.shape[:2]
    return jnp.swapaxes(a.reshape(B, S // d, d, *a.shape[2:]), 1, 2)


def kernel(x, mem, positions, norm_mix, w_in, da_lambda, da_subln, conv_w, w_branch, w_gate, b_gate,
           w_out, norm_cross, norm_mem, wq_x, wkv_x, wo_x, norm_moe, w_router, w_e1, w_e3, w_e2,
           norm_final):
    B, S, D = x.shape
    depth = w_in.shape[0]
    E = w_router.shape[-1]
    cap = CAPACITY_FACTOR * S // E
    T = B * S

    ca, sa = _rope_tables(positions, DA_QK_DIM, LANES // DA_QK_DIM)
    tabs_b = [_rope_tables(_by_class(positions, d), DIL_HEAD_DIM, 1) for d in DILATIONS]
    kv_all = memory_kv(mem, norm_mem, wkv_x.astype(BF16))

    hx = x
    for l in range(depth):
        lam_init = jnp.full((1, 1), 0.8 - 0.6 * math.exp(-0.3 * l), F32)
        w_in_l = w_in[l].astype(BF16)
        xn, *xn_cls = rmsnorm_with_classes(hx, norm_mix[l])
        xn = xn.reshape(T, D)
        proj3 = in_projection(xn, w_in_l, ca, sa, *tabs_b[0]).reshape(B, S, NAT_COLS)
        ya = diff_attention(proj3, da_lambda[l], lam_init, da_subln[l]).reshape(T, A_WIDTH)
        outs, lses = [], []
        for g, (window, d) in enumerate(DIL_PAIRS):
            if d == 1:
                qkv, first = proj3.reshape(B, 1, S, NAT_COLS), B_FIRST_TILE
            else:
                xg = xn_cls[[dd for dd in DILATIONS if dd > 1].index(d)].reshape(T, D)
                qkv = in_projection(xg, w_in_l, ca, sa, *tabs_b[g], group=g)
                qkv, first = qkv.reshape(B, d, S // d, B_TILES_PER_GROUP * COL_TILE), 0
            o, lse = dilated_attention(qkv, first, window, d, g)
            outs.append(o)
            lses.append(lse)
        yb = merge_groups(outs, lses).reshape(T, B_WIDTH)
        yc = short_conv_branch(proj3, conv_w[l]).reshape(T, C_WIDTH)
        merged = gate_merge(xn, ya, yb, yc, w_gate[l].astype(BF16), b_gate[l].reshape(1, -1),
                            w_branch[l].astype(BF16))
        h = matmul_residual(merged, w_out[l].astype(BF16), hx.reshape(T, -1))
        hx, aff_t = cross_attention_router(
            h.reshape(B, S, D), norm_cross[l], wq_x[l].astype(BF16), kv_all[l],
            wo_x[l].astype(BF16), norm_moe[l], w_router[l].T)
        idx, gate = expert_choice_select(aff_t, cap)
        hx = expert_ffn(idx, gate, hx.reshape(T, -1), w_e1[l].astype(BF16),
                        w_e3[l].astype(BF16), w_e2[l].astype(BF16), S).reshape(B, S, -1)
    return rmsnorm(hx.reshape(T, -1), norm_final, F32).reshape(B, S, D)
```

```python
import functools
import math

import jax
import jax.numpy as jnp
from jax import lax
from jax.experimental import pallas as pl
from jax.experimental.pallas import tpu as pltpu

F32 = jnp.float32
BF16 = jnp.bfloat16

EPS = 1e-6
ROPE_THETA = 10000.0
DA_HEADS = 4
DA_QK_DIM = 64
DA_V_DIM = 2 * DA_QK_DIM
DIL_PAIRS = ((128, 1), (512, 4), (2048, 16))
DIL_HEADS = 4
DIL_HEAD_DIM = 128
CONV_CH = 1024
A_WIDTH = DA_HEADS * DA_V_DIM
B_WIDTH = DIL_HEADS * DIL_HEAD_DIM
C_WIDTH = CONV_CH
MIX_WIDTH = A_WIDTH + B_WIDTH + C_WIDTH
N_BRANCH = 3
IN_COLS = 3 * A_WIDTH + len(DIL_PAIRS) * 3 * B_WIDTH + 3 * CONV_CH
X_HEADS = 4
X_HEAD_DIM = 128
CAPACITY_FACTOR = 2

LANES = 128
V7X_VMEM_BYTES = 64 * 1024 * 1024
VMEM_LIMIT = V7X_VMEM_BYTES - 8 * 1024 * 1024
COL_TILE = 512
BF16_ROWS = 16
DMA_PRIORITIES = 2
DIL_QBLK = 256
NEG_BIG = -1e30


def _params(*sem):
    return pltpu.CompilerParams(dimension_semantics=sem, vmem_limit_bytes=VMEM_LIMIT)


def _rms(x, g):
    return x * lax.rsqrt(jnp.mean(x * x, axis=-1, keepdims=True) + EPS) * g


def _dot(a, b):
    return jnp.dot(a, b, preferred_element_type=F32)


def _dot_nt(a, b, **kw):
    return lax.dot_general(a, b, (((1,), (1,)), ((), ())), preferred_element_type=F32, **kw)


def _rms_kernel(x_ref, g_ref, o_ref):
    o_ref[...] = _rms(x_ref[...], g_ref[...]).astype(o_ref.dtype)


def rmsnorm(x, g, out_dtype):
    T = x.shape[0]
    D = g.shape[0]
    tm = min(512, T)
    return pl.pallas_call(
        _rms_kernel,
        grid=(T // tm,),
        in_specs=[pl.BlockSpec((tm, D), lambda i: (i, 0)),
                  pl.BlockSpec((1, D), lambda i: (0, 0))],
        out_specs=pl.BlockSpec((tm, D), lambda i: (i, 0)),
        out_shape=jax.ShapeDtypeStruct((T, D), out_dtype),
        compiler_params=_params("parallel"),
        name="rmsnorm",
    )(x, g.reshape(1, D))


DILATIONS = tuple(d for _, d in DIL_PAIRS)


def _rms_classes_kernel(x_ref, g_ref, o_ref, *rest):
    class_refs, scr = rest[:-1], rest[-1]
    y = _rms(x_ref[0], g_ref[...])
    o_ref[0] = y.astype(o_ref.dtype)
    tm, D = y.shape
    for c in range(D // LANES):
        cs = slice(c * LANES, (c + 1) * LANES)
        scr[c] = y[:, cs]
        for ref, d in zip(class_refs, [d for d in DILATIONS if d > 1]):
            for r in range(d):
                ref[0, r, :, cs] = scr[c, pl.ds(r, tm // d, stride=d), :].astype(ref.dtype)


def rmsnorm_with_classes(x3, g):
    B, S, _ = x3.shape
    D = g.shape[0]
    tm = min(512, S)
    ds = [d for d in DILATIONS if d > 1]
    return pl.pallas_call(
        _rms_classes_kernel,
        grid=(B, S // tm),
        in_specs=[pl.BlockSpec((1, tm, D), lambda b, i: (b, i, 0)),
                  pl.BlockSpec((1, D), lambda b, i: (0, 0))],
        out_specs=[pl.BlockSpec((1, tm, D), lambda b, i: (b, i, 0))]
        + [pl.BlockSpec((1, d, tm // d, D), lambda b, i: (b, 0, i, 0)) for d in ds],
        out_shape=[jax.ShapeDtypeStruct((B, S, D), BF16)]
        + [jax.ShapeDtypeStruct((B, d, S // d, D), BF16) for d in ds],
        scratch_shapes=[pltpu.VMEM((D // LANES, tm, LANES), F32)],
        compiler_params=_params("parallel", "parallel"),
        name="rmsnorm_classes",
    )(x3, g.reshape(1, D))


N_COL_TILES = IN_COLS // COL_TILE
A_ROPE_TILES = 2 * A_WIDTH // COL_TILE
B_FIRST_TILE = 3 * A_WIDTH // COL_TILE
B_TILES_PER_GROUP = 3 * B_WIDTH // COL_TILE
B_ROPE_PER_GROUP = 2 * B_WIDTH // COL_TILE
C_FIRST_TILE = B_FIRST_TILE + len(DIL_PAIRS) * B_TILES_PER_GROUP


def _inproj_kernel(x_ref, w_ref, ca_ref, sa_ref, cb_ref, sb_ref, o_ref, *, col_map):
    j = col_map(pl.program_id(1))
    acc = _dot(x_ref[...], w_ref[...])
    jb = j - B_FIRST_TILE
    is_a = j < A_ROPE_TILES
    is_b = (j >= B_FIRST_TILE) & (j < C_FIRST_TILE) & (jb % B_TILES_PER_GROUP < B_ROPE_PER_GROUP)

    @pl.when(is_a)
    def _():
        c, s = ca_ref[...], sa_ref[...]
        lane = lax.broadcasted_iota(jnp.int32, c.shape, 1)
        first = (lane % DA_QK_DIM) < (DA_QK_DIM // 2)
        for t in range(COL_TILE // LANES):
            x = acc[:, t * LANES:(t + 1) * LANES]
            rot = jnp.where(first, pltpu.roll(x, LANES - DA_QK_DIM // 2, 1),
                            pltpu.roll(x, DA_QK_DIM // 2, 1))
            o_ref[:, t * LANES:(t + 1) * LANES] = (x * c + rot * s).astype(o_ref.dtype)

    @pl.when(is_b)
    def _():
        c, s = cb_ref[...], sb_ref[...]
        for t in range(COL_TILE // LANES):
            x = acc[:, t * LANES:(t + 1) * LANES]
            rot = pltpu.roll(x, DIL_HEAD_DIM // 2, 1)
            o_ref[:, t * LANES:(t + 1) * LANES] = (x * c + rot * s).astype(o_ref.dtype)

    @pl.when(jnp.logical_not(is_a | is_b))
    def _():
        o_ref[...] = acc.astype(o_ref.dtype)


NAT_TILES = N_COL_TILES - (len(DIL_PAIRS) - 1) * B_TILES_PER_GROUP
NAT_COLS = NAT_TILES * COL_TILE
NAT_C_FIRST_TILE = B_FIRST_TILE + B_TILES_PER_GROUP


def _natural_col_map(j):
    return j + jnp.where(j >= NAT_C_FIRST_TILE, (len(DIL_PAIRS) - 1) * B_TILES_PER_GROUP, 0)


def in_projection(xn, w, ca, sa, cb, sb, group=None):
    T, D = xn.shape
    tm = min(2048, T)
    if group is None:
        n_tiles, col_map = NAT_TILES, _natural_col_map
    else:
        first = B_FIRST_TILE + group * B_TILES_PER_GROUP
        n_tiles, col_map = B_TILES_PER_GROUP, lambda j: j + first
    tab = pl.BlockSpec((tm, LANES), lambda i, j: (i, 0))
    return pl.pallas_call(
        functools.partial(_inproj_kernel, col_map=col_map),
        grid=(T // tm, n_tiles),
        in_specs=[pl.BlockSpec((tm, D), lambda i, j: (i, 0)),
                  pl.BlockSpec((D, COL_TILE), lambda i, j: (0, col_map(j))),
                  tab, tab, tab, tab],
        out_specs=pl.BlockSpec((tm, COL_TILE), lambda i, j: (i, j)),
        out_shape=jax.ShapeDtypeStruct((T, n_tiles * COL_TILE), BF16),
        compiler_params=_params("parallel", "arbitrary"),
        name="in_projection" if group is None else f"in_projection_g{group}",
    )(xn, w, ca, sa, cb, sb)


KV_UNROLL = 8


def _diff_attn_kernel(q_ref, k_ref, v_ref, lam_ref, li_ref, g_ref, o_ref, vext_ref, *, tk):
    S = k_ref.shape[1]
    tq = q_ref.shape[1]

    @pl.when(pl.program_id(2) == 0)
    def _():
        vext_ref[:, :DA_V_DIM] = v_ref[0]
        vext_ref[:, DA_V_DIM:] = jnp.ones((S, DA_V_DIM), BF16)

    q = q_ref[0] * jnp.asarray(DA_QK_DIM ** -0.5, BF16)
    lane = lax.broadcasted_iota(jnp.int32, q.shape, 1)
    zero = jnp.zeros_like(q)
    q0 = jnp.where(lane < DA_QK_DIM, q, zero)
    q1 = jnp.where(lane >= DA_QK_DIM, q, zero)

    def step(qm, kj, vj, m, a):
        s = _dot_nt(qm, kj)
        m_new = jnp.maximum(m, jnp.max(s, axis=-1, keepdims=True))
        p = jnp.exp((s - m_new).astype(BF16))
        a = jnp.exp(m - m_new) * a + _dot(p, vj)
        return m_new, a

    def body(j, carry):
        m0, a0, m1, a1 = carry
        off = pl.multiple_of(j * tk, tk)
        kj = k_ref[0, pl.ds(off, tk), :]
        vj = vext_ref[pl.ds(off, tk), :]
        m0, a0 = step(q0, kj, vj, m0, a0)
        m1, a1 = step(q1, kj, vj, m1, a1)
        return m0, a0, m1, a1

    mi = jnp.full((tq, 1), NEG_BIG, F32)
    ai = jnp.zeros((tq, 2 * DA_V_DIM), F32)
    _, a0, _, a1 = lax.fori_loop(0, S // tk, body, (mi, ai, mi, ai), unroll=KV_UNROLL)

    lp = lam_ref[...]
    lam_init = li_ref[...]
    lam = (jnp.exp(jnp.sum(lp[0:1] * lp[1:2], axis=-1, keepdims=True))
           - jnp.exp(jnp.sum(lp[2:3] * lp[3:4], axis=-1, keepdims=True)) + lam_init)
    o = a0[:, :DA_V_DIM] / a0[:, DA_V_DIM:] - lam * (a1[:, :DA_V_DIM] / a1[:, DA_V_DIM:])
    o_ref[0] = (_rms(o, g_ref[...]) * (1.0 - lam_init)).astype(o_ref.dtype)


def diff_attention(proj3, lam_p, lam_init, sub_gain):
    B, S, _ = proj3.shape
    tq = min(1024, S)
    tk = min(512, S)
    assert (S // tk) % KV_UNROLL == 0
    kb = A_WIDTH // DA_V_DIM
    return pl.pallas_call(
        functools.partial(_diff_attn_kernel, tk=tk),
        grid=(B, DA_HEADS, S // tq),
        in_specs=[pl.BlockSpec((1, tq, DA_V_DIM), lambda b, h, i: (b, i, h)),
                  pl.BlockSpec((1, S, DA_V_DIM), lambda b, h, i: (b, 0, kb + h)),
                  pl.BlockSpec((1, S, DA_V_DIM), lambda b, h, i: (b, 0, 2 * kb + h)),
                  pl.BlockSpec((4, DA_QK_DIM), lambda b, h, i: (0, 0)),
                  pl.BlockSpec((1, 1), lambda b, h, i: (0, 0)),
                  pl.BlockSpec((1, DA_V_DIM), lambda b, h, i: (0, 0))],
        out_specs=pl.BlockSpec((1, tq, DA_V_DIM), lambda b, h, i: (b, i, h)),
        out_shape=jax.ShapeDtypeStruct((B, S, A_WIDTH), BF16),
        scratch_shapes=[pltpu.VMEM((S, 2 * DA_V_DIM), BF16)],
        compiler_params=_params("parallel", "parallel", "arbitrary"),
        name="diff_attention",
    )(proj3, proj3, proj3, lam_p, lam_init, sub_gain.reshape(1, DA_V_DIM))


def _dilated_kernel(q_ref, k_ref, v_ref, o_ref, lse_ref, *, n_side):
    i = pl.program_id(2)
    tq = q_ref.shape[2]
    L = k_ref.shape[2]
    W = tq + 2 * n_side
    start = pl.multiple_of(jnp.clip(i * tq - n_side, 0, L - W), n_side)
    qpos = i * tq + lax.broadcasted_iota(jnp.int32, (tq, W), 0)
    kpos = start + lax.broadcasted_iota(jnp.int32, (tq, W), 1)
    valid = jnp.abs(kpos - qpos) <= n_side
    scale = DIL_HEAD_DIM ** -0.5
    for h in range(DIL_HEADS):
        hs = slice(h * DIL_HEAD_DIM, (h + 1) * DIL_HEAD_DIM)
        kw = k_ref[0, 0, pl.ds(start, W), hs]
        vw = v_ref[0, 0, pl.ds(start, W), hs]
        s = jnp.where(valid, _dot_nt(q_ref[0, 0, :, hs], kw) * scale, NEG_BIG)
        m = jnp.max(s, axis=-1, keepdims=True)
        p = jnp.exp((s - m).astype(BF16))
        ol = _dot(p, jnp.concatenate([vw, jnp.ones_like(vw)], axis=1))
        l = ol[:, DIL_HEAD_DIM:]
        o_ref[0, 0, :, hs] = ol[:, :DIL_HEAD_DIM] / l
        lse_ref[0, 0, :, hs] = m + jnp.log(l)


def dilated_attention(qkv, first_tile, window, dilation, group):
    B, d, L, _ = qkv.shape
    n_side = (window // 2) // dilation
    tq = next(t for t in (DIL_QBLK, DIL_QBLK // 2) if t + 2 * n_side <= L and L % t == 0)
    assert d == dilation and n_side % BF16_ROWS == 0

    def resident(kind):
        return pl.BlockSpec((1, 1, L, B_WIDTH), lambda b, r, i: (b, r, 0, first_tile + kind))

    out_spec = pl.BlockSpec((1, 1, tq, B_WIDTH), lambda b, r, i: (b, r, i, 0))
    return pl.pallas_call(
        functools.partial(_dilated_kernel, n_side=n_side),
        grid=(B, d, L // tq),
        in_specs=[pl.BlockSpec((1, 1, tq, B_WIDTH), lambda b, r, i: (b, r, i, first_tile)),
                  resident(1), resident(2)],
        out_specs=[out_spec, out_spec],
        out_shape=[jax.ShapeDtypeStruct((B, d, L, B_WIDTH), F32)] * 2,
        compiler_params=_params("parallel", "parallel", "arbitrary"),
        name=f"dilated_attention_g{group}",
    )(qkv, qkv, qkv)


def _merge_kernel(*refs):
    n = len(DILATIONS)
    ins, y_ref, scr = refs[:2 * n], refs[2 * n], refs[2 * n + 1:]
    tm = y_ref.shape[1]
    for c in range(B_WIDTH // LANES):
        cs = slice(c * LANES, (c + 1) * LANES)
        vals, k = [], 0
        for g, d in enumerate(DILATIONS):
            o_ref, l_ref = ins[2 * g], ins[2 * g + 1]
            if d == 1:
                vals.append((o_ref[0, 0, :, cs], l_ref[0, 0, :, cs]))
                continue
            so, sl = scr[2 * k], scr[2 * k + 1]
            k += 1
            for r in range(d):
                so[c, pl.ds(r, tm // d, stride=d), :] = o_ref[0, r, :, cs]
                sl[c, pl.ds(r, tm // d, stride=d), :] = l_ref[0, r, :, cs]
            vals.append((so[c], sl[c]))
        m = functools.reduce(jnp.maximum, [l for _, l in vals])
        ws = [jnp.exp(l - m) for _, l in vals]
        num = sum(w * o for w, (o, _) in zip(ws, vals))
        y_ref[0, :, cs] = (num / sum(ws)).astype(y_ref.dtype)


def merge_groups(outs, lses):
    B, _, S, _ = outs[0].shape
    tm = min(512, S)
    specs, args = [], []
    for d, o, l in zip(DILATIONS, outs, lses):
        sp = pl.BlockSpec((1, d, tm // d, B_WIDTH), lambda b, i: (b, 0, i, 0))
        specs += [sp, sp]
        args += [o, l]
    n_scr = 2 * sum(1 for d in DILATIONS if d > 1)
    return pl.pallas_call(
        _merge_kernel,
        grid=(B, S // tm),
        in_specs=specs,
        out_specs=pl.BlockSpec((1, tm, B_WIDTH), lambda b, i: (b, i, 0)),
        out_shape=jax.ShapeDtypeStruct((B, S, B_WIDTH), BF16),
        scratch_shapes=[pltpu.VMEM((B_WIDTH // LANES, tm, LANES), F32)] * n_scr,
        compiler_params=_params("parallel", "parallel"),
        name="merge_groups",
    )(*args)


HALO_ROWS = 16


def _conv_kernel(h_ref, gb_ref, gc_ref, hp_ref, gp_ref, hn_ref, gn_ref, w_ref, o_ref):
    i = pl.program_id(1)
    ts = h_ref.shape[1]
    u = gc_ref[0].astype(F32) * h_ref[0].astype(F32)
    up = (gp_ref[0].astype(F32) * hp_ref[0].astype(F32))[HALO_ROWS - 1:HALO_ROWS]
    un = (gn_ref[0].astype(F32) * hn_ref[0].astype(F32))[0:1]
    up = jnp.where(i > 0, up, 0.0)
    un = jnp.where(i < pl.num_programs(1) - 1, un, 0.0)
    rows = lax.broadcasted_iota(jnp.int32, u.shape, 0)
    u_prev = jnp.where(rows == 0, up, pltpu.roll(u, 1, 0))
    u_next = jnp.where(rows == ts - 1, un, pltpu.roll(u, ts - 1, 0))
    w = w_ref[...]
    y = gb_ref[0].astype(F32) * (w[0:1] * u_prev + w[1:2] * u + w[2:3] * u_next)
    o_ref[0] = y.astype(o_ref.dtype)


def short_conv_branch(proj3, conv_w):
    B, S, cols = proj3.shape
    ts = min(512, S)
    c0 = (cols - 3 * CONV_CH) // CONV_CH
    hb = ts // HALO_ROWS
    n_halo = S // HALO_ROWS

    def main(c):
        return pl.BlockSpec((1, ts, CONV_CH), lambda b, i: (b, i, c))

    def prev(c):
        return pl.BlockSpec((1, HALO_ROWS, CONV_CH), lambda b, i: (b, jnp.maximum(i * hb - 1, 0), c))

    def nxt(c):
        return pl.BlockSpec((1, HALO_ROWS, CONV_CH),
                            lambda b, i: (b, jnp.minimum((i + 1) * hb, n_halo - 1), c))

    return pl.pallas_call(
        _conv_kernel,
        grid=(B, S // ts),
        in_specs=[main(c0), main(c0 + 1), main(c0 + 2), prev(c0), prev(c0 + 2), nxt(c0), nxt(c0 + 2),
                  pl.BlockSpec((3, CONV_CH), lambda b, i: (0, 0))],
        out_specs=pl.BlockSpec((1, ts, CONV_CH), lambda b, i: (b, i, 0)),
        out_shape=jax.ShapeDtypeStruct((B, S, CONV_CH), BF16),
        compiler_params=_params("parallel", "parallel"),
        name="short_conv",
    )(proj3, proj3, proj3, proj3, proj3, proj3, proj3, conv_w)


def _gate_merge_kernel(xn_ref, ya_ref, yb_ref, yc_ref, wg0_ref, wg1_ref, wg2_ref,
                       b0_ref, b1_ref, b2_ref, wa_ref, wb_ref, wc_ref, o_ref):
    xn = xn_ref[...]
    acc = jax.nn.sigmoid(_dot(xn, wg0_ref[...]) + b0_ref[...]) * _dot(ya_ref[...], wa_ref[...])
    acc += jax.nn.sigmoid(_dot(xn, wg1_ref[...]) + b1_ref[...]) * _dot(yb_ref[...], wb_ref[...])
    acc += jax.nn.sigmoid(_dot(xn, wg2_ref[...]) + b2_ref[...]) * _dot(yc_ref[...], wc_ref[...])
    o_ref[...] = acc.astype(o_ref.dtype)


def gate_merge(xn, ya, yb, yc, w_gate, b_gate, w_br):
    T, D = xn.shape
    tm = min(1024, T)
    tn = min(512, D)
    nj = D // tn
    assert A_WIDTH == B_WIDTH and C_WIDTH == 2 * A_WIDTH

    def wg(k):
        return pl.BlockSpec((D, tn), lambda i, j: (0, k * nj + j))

    def bg(k):
        return pl.BlockSpec((1, tn), lambda i, j: (0, k * nj + j))

    def rowblk(w):
        return pl.BlockSpec((tm, w), lambda i, j: (i, 0))

    return pl.pallas_call(
        _gate_merge_kernel,
        grid=(T // tm, nj),
        in_specs=[rowblk(D), rowblk(A_WIDTH), rowblk(B_WIDTH), rowblk(C_WIDTH),
                  wg(0), wg(1), wg(2), bg(0), bg(1), bg(2),
                  pl.BlockSpec((A_WIDTH, tn), lambda i, j: (0, j)),
                  pl.BlockSpec((B_WIDTH, tn), lambda i, j: (1, j)),
                  pl.BlockSpec((C_WIDTH, tn), lambda i, j: (1, j))],
        out_specs=pl.BlockSpec((tm, tn), lambda i, j: (i, j)),
        out_shape=jax.ShapeDtypeStruct((T, D), BF16),
        compiler_params=_params("parallel", "arbitrary"),
        name="gate_merge",
    )(xn, ya, yb, yc, w_gate, w_gate, w_gate, b_gate, b_gate, b_gate, w_br, w_br, w_br)


def _matmul_residual_kernel(x_ref, w_ref, r_ref, o_ref):
    o_ref[...] = r_ref[...] + _dot(x_ref[...], w_ref[...])


def matmul_residual(x, w, res):
    T, K = x.shape
    N = w.shape[1]
    tm = min(512, T)
    tn = min(2048, N)
    return pl.pallas_call(
        _matmul_residual_kernel,
        grid=(T // tm, N // tn),
        in_specs=[pl.BlockSpec((tm, K), lambda i, j: (i, 0)),
                  pl.BlockSpec((K, tn), lambda i, j: (0, j)),
                  pl.BlockSpec((tm, tn), lambda i, j: (i, j))],
        out_specs=pl.BlockSpec((tm, tn), lambda i, j: (i, j)),
        out_shape=jax.ShapeDtypeStruct((T, N), F32),
        compiler_params=_params("parallel", "arbitrary"),
        name="matmul_residual",
    )(x, w, res)


def _memkv_kernel(mem_ref, g_ref, w_ref, o_ref):
    o_ref[0, 0] = _dot(_rms(mem_ref[0], g_ref[0]).astype(BF16), w_ref[0]).astype(o_ref.dtype)


def memory_kv(mem, norm_mem, wkv):
    B, M, D = mem.shape
    L, _, N = wkv.shape
    return pl.pallas_call(
        _memkv_kernel,
        grid=(L, B),
        in_specs=[pl.BlockSpec((1, M, D), lambda l, b: (b, 0, 0)),
                  pl.BlockSpec((1, 1, D), lambda l, b: (l, 0, 0)),
                  pl.BlockSpec((1, D, N), lambda l, b: (l, 0, 0))],
        out_specs=pl.BlockSpec((1, 1, M, N), lambda l, b: (l, b, 0, 0)),
        out_shape=jax.ShapeDtypeStruct((L, B, M, N), BF16),
        compiler_params=_params("parallel", "parallel"),
        name="memory_kv",
    )(mem, norm_mem.reshape(L, 1, D), wkv)


def _pack_bf16_pairs(x):
    m = x.shape[1] // 2
    hi = pltpu.bitcast(x[:, :m].astype(BF16).astype(F32), jnp.uint32)
    lo = pltpu.bitcast(x[:, m:].astype(BF16).astype(F32), jnp.uint32)
    return pltpu.bitcast(hi | (lo >> 16), F32)


def _unpack_bf16_pairs(w):
    bits = pltpu.bitcast(w, jnp.uint32)
    hi = pltpu.bitcast(bits & jnp.uint32(0xFFFF0000), F32).astype(BF16)
    lo = pltpu.bitcast(bits << 16, F32).astype(BF16)
    return jnp.concatenate([hi, lo], axis=1)


def _cross_kernel(h_ref, gx_ref, wq_ref, kv_ref, wo_ref, gm_ref, wr_ref, hx_ref, aff_ref):
    h = h_ref[0]
    D = h.shape[1]
    q = _dot(_rms(h, gx_ref[...]).astype(BF16), wq_ref[...]) * (X_HEAD_DIM ** -0.5)
    q = q.astype(BF16)
    kv = kv_ref[0]
    outs = []
    for hd in range(X_HEADS):
        ks = slice(hd * X_HEAD_DIM, (hd + 1) * X_HEAD_DIM)
        vs = slice((X_HEADS + hd) * X_HEAD_DIM, (X_HEADS + hd + 1) * X_HEAD_DIM)
        s = _dot_nt(q[:, ks], kv[:, ks])
        e = jnp.exp(s - jnp.max(s, axis=-1, keepdims=True))
        p = e / jnp.sum(e, axis=-1, keepdims=True)
        outs.append(_dot(p.astype(BF16), kv[:, vs]))
    o = jnp.concatenate(outs, axis=1).astype(BF16)
    hn = h + _dot(o, wo_ref[...])
    xn = _rms(hn, gm_ref[...])
    hx_ref[0, :, :D] = hn
    hx_ref[0, :, D:] = _pack_bf16_pairs(xn)
    lt = _dot_nt(wr_ref[...], xn, precision=lax.Precision.HIGHEST)
    e = jnp.exp(lt - jnp.max(lt, axis=0, keepdims=True))
    aff_ref[0] = e / jnp.sum(e, axis=0, keepdims=True)


def cross_attention_router(h3, g_cross, wq, kv, wo, g_moe, w_router_t):
    B, S, D = h3.shape
    M, NKV = kv.shape[1:]
    E = w_router_t.shape[0]
    NQ = wq.shape[1]
    tm = min(512, S)

    def full(shape):
        return pl.BlockSpec(shape, lambda b, i: (0,) * len(shape))

    return pl.pallas_call(
        _cross_kernel,
        grid=(B, S // tm),
        in_specs=[pl.BlockSpec((1, tm, D), lambda b, i: (b, i, 0)),
                  full((1, D)), full((D, NQ)),
                  pl.BlockSpec((1, M, NKV), lambda b, i: (b, 0, 0)),
                  full((NQ, D)), full((1, D)), full((E, D))],
        out_specs=[pl.BlockSpec((1, tm, D + D // 2), lambda b, i: (b, i, 0)),
                   pl.BlockSpec((1, E, tm), lambda b, i: (b, 0, i))],
        out_shape=[jax.ShapeDtypeStruct((B, S, D + D // 2), F32),
                   jax.ShapeDtypeStruct((B, E, S), F32)],
        compiler_params=_params("parallel", "parallel"),
        name="cross_attention_router",
    )(h3, g_cross.reshape(1, D), wq, kv, wo, g_moe.reshape(1, D), w_router_t)


SLOT_BLK = 128


def _select_kernel(aff_ref, cs_ref, w_ref, ts_ref, first_ref, *, cap):
    E, C, _ = aff_ref.shape[1:]
    R = E * C
    a = aff_ref[0].reshape(R, LANES)
    bits = pltpu.bitcast(a, jnp.int32)
    rid = lax.broadcasted_iota(jnp.int32, (R, LANES), 0)

    ii = lax.broadcasted_iota(jnp.int32, (LANES, LANES), 0)
    jj = lax.broadcasted_iota(jnp.int32, (LANES, LANES), 1)
    upper = jnp.where(ii <= jj, 1.0, 0.0).astype(BF16)
    ones = jnp.ones((LANES, LANES), BF16)
    ri = lax.broadcasted_iota(jnp.int32, (R, R), 0) // C
    rj = lax.broadcasted_iota(jnp.int32, (R, R), 1)
    same_expert = jnp.where(rj // C == ri, 1.0, 0.0).astype(BF16)
    rows_before = jnp.where((rj // C == ri) & (rj < lax.broadcasted_iota(jnp.int32, (R, R), 0)),
                            1.0, 0.0).astype(BF16)

    def row_totals(x):
        return _dot(x.astype(BF16), ones).astype(BF16)

    def count(x):
        return _dot(same_expert, row_totals(x))

    def prefix_count(x):
        return _dot(x.astype(BF16), upper) + _dot(rows_before, row_totals(x))

    thr = jnp.zeros((R, LANES), jnp.int32)
    for bit in range(30, -1, -1):
        cand = thr | jnp.int32(1 << bit)
        thr = jnp.where(count(jnp.where(bits >= cand, 1.0, 0.0)) >= cap, cand, thr)
    gt_f = jnp.where(bits > thr, 1.0, 0.0)
    eq_f = jnp.where(bits == thr, 1.0, 0.0)

    need = cap - count(gt_f)
    eq_rank = prefix_count(eq_f) - eq_f
    sel = gt_f + eq_f * jnp.where(eq_rank < need, 1.0, 0.0)
    csum = prefix_count(sel)
    tok = ((rid % C) * LANES + lax.broadcasted_iota(jnp.int32, (R, LANES), 1)).astype(F32)
    cs_ref[0] = csum
    w_ref[0] = a * sel
    ts_ref[0] = tok * sel
    before = _dot(rows_before, row_totals(sel))
    for blk in range(R // LANES):
        d = jnp.where(ii == jj, before[blk * LANES:(blk + 1) * LANES], 0.0)
        first_ref[0, blk:blk + 1, :] = jnp.sum(d, axis=0, keepdims=True).astype(jnp.int32)


def _resolve_kernel(first_ref, cs_ref, w_ref, ts_ref, idx_ref, gate_ref, acc_i, acc_g, *, cap):
    b = pl.program_id(0)
    e = pl.program_id(1)
    C = cs_ref.shape[1]
    n_blk = cap // SLOT_BLK
    acc_i[...] = jnp.zeros_like(acc_i)
    acc_g[...] = jnp.zeros_like(acc_g)

    def chunk(c, _):
        p0 = first_ref[b, e * C + c]
        p1 = jnp.where(c + 1 < C, first_ref[b, e * C + jnp.minimum(c + 1, C - 1)], cap)
        cs_row = cs_ref[0, pl.ds(c, 1), :]
        ts_row = ts_ref[0, pl.ds(c, 1), :]
        w_row = w_ref[0, pl.ds(c, 1), :]
        for j in range(LANES // SLOT_BLK + 1):
            blk = p0 // SLOT_BLK + j

            @pl.when((blk * SLOT_BLK < p1) & (blk < n_blk))
            def _():
                off = pl.multiple_of(blk * SLOT_BLK, SLOT_BLK)
                k1 = (lax.broadcasted_iota(jnp.int32, (SLOT_BLK, LANES), 0) + (off + 1)).astype(F32)
                hit = cs_row == k1
                rows = pl.ds(off, SLOT_BLK)
                acc_i[rows, :] = acc_i[rows, :] + jnp.where(hit, ts_row, 0.0)
                acc_g[rows, :] = acc_g[rows, :] + jnp.where(hit, w_row, 0.0)
        return 0

    lax.fori_loop(0, C, chunk, 0)
    idx_ref[0, 0] = jnp.sum(acc_i[...], axis=1, keepdims=True).astype(jnp.int32)
    gate_ref[0, 0] = jnp.sum(acc_g[...], axis=1, keepdims=True)


def expert_choice_select(aff_t, cap):
    B, E, S = aff_t.shape
    C = S // LANES
    R = E * C
    assert cap % SLOT_BLK == 0 and R % LANES == 0
    rows = pl.BlockSpec((1, R, LANES), lambda b: (b, 0, 0))
    cs, w, ts, first = pl.pallas_call(
        functools.partial(_select_kernel, cap=cap),
        grid=(B,),
        in_specs=[pl.BlockSpec((1, E, C, LANES), lambda b: (b, 0, 0, 0))],
        out_specs=[rows, rows, rows, pl.BlockSpec((1, R // LANES, LANES), lambda b: (b, 0, 0))],
        out_shape=[jax.ShapeDtypeStruct((B, R, LANES), F32)] * 3
        + [jax.ShapeDtypeStruct((B, R // LANES, LANES), jnp.int32)],
        compiler_params=_params("parallel"),
        name="expert_choice_select",
    )(aff_t.reshape(B, E, C, LANES))

    chunk_rows = pl.BlockSpec((1, C, LANES), lambda b, e, first: (b * E + e, 0, 0))
    slots = pl.BlockSpec((1, 1, cap, 1), lambda b, e, first: (b, e, 0, 0))
    return pl.pallas_call(
        functools.partial(_resolve_kernel, cap=cap),
        grid_spec=pltpu.PrefetchScalarGridSpec(
            num_scalar_prefetch=1,
            grid=(B, E),
            in_specs=[chunk_rows] * 3,
            out_specs=[slots, slots],
            scratch_shapes=[pltpu.VMEM((cap, LANES), F32)] * 2),
        out_shape=[jax.ShapeDtypeStruct((B, E, cap, 1), jnp.int32),
                   jax.ShapeDtypeStruct((B, E, cap, 1), F32)],
        compiler_params=_params("parallel", "parallel"),
        name="expert_choice_resolve",
    )(first.reshape(B, R), *[x.reshape(B * E, C, LANES) for x in (cs, w, ts)])


FFN_SLOTS = 4
FFN_AHEAD = 2


def _expert_ffn_kernel(idx_hbm, gate_ref, hx_in_hbm, w1_ref, w3_ref, w2_ref, hx_hbm,
                       idx_smem, buf, sems, *, sub, seq_len):
    del hx_in_hbm
    e = pl.program_id(0)
    n_exp = pl.num_programs(0)
    n_seq = gate_ref.shape[0]
    cap = idx_smem.shape[0] // n_seq
    D = w2_ref.shape[2]
    per_seq = cap // sub
    n_tiles = n_seq * per_seq
    for b in range(n_seq):
        cp = pltpu.make_async_copy(idx_hbm.at[pl.ds((b * n_exp + e) * cap, cap)],
                                   idx_smem.at[pl.ds(b * cap, cap)], sems.at[0])
        cp.start()
        cp.wait()

    def token_row(t, r):
        return pl.ds((t // per_seq) * seq_len + idx_smem[t * sub + r], 1)

    def gather_row(t, slot, r, priority):
        pltpu.make_async_copy(hx_hbm.at[token_row(t, r), :], buf.at[slot, pl.ds(r, 1), :],
                              sems.at[1 + slot]).start(priority=priority)

    def wait_gather(slot):
        pltpu.make_async_copy(hx_hbm.at[pl.ds(0, sub), :], buf.at[slot], sems.at[1 + slot]).wait()

    def write_back(t, slot):
        for r in range(sub):
            pltpu.make_async_copy(buf.at[slot, pl.ds(r, 1), pl.ds(0, D)],
                                  hx_hbm.at[token_row(t, r), pl.ds(0, D)],
                                  sems.at[1 + FFN_SLOTS + slot]).start(priority=r % DMA_PRIORITIES)

    def wait_write_back(slot):
        pltpu.make_async_copy(buf.at[slot, :, pl.ds(0, D)], hx_hbm.at[pl.ds(0, sub), pl.ds(0, D)],
                              sems.at[1 + FFN_SLOTS + slot]).wait()

    def tile(t, fetch_ahead):
        slot = t % FFN_SLOTS
        wait_gather(slot)
        if fetch_ahead:
            ahead = (t + FFN_AHEAD) % FFN_SLOTS

            @pl.when(t >= FFN_SLOTS - FFN_AHEAD)
            def _():
                wait_write_back(ahead)
            for r in range(sub):
                gather_row(t + FFN_AHEAD, ahead, r, r % DMA_PRIORITIES)
        xg = _unpack_bf16_pairs(buf[slot, :, D:])
        a = _dot(xg, w1_ref[0])
        c = _dot(xg, w3_ref[0])
        hid = (a * jax.nn.sigmoid(a) * c).astype(BF16)
        y = _dot(hid, w2_ref[0])
        off = pl.multiple_of((t % per_seq) * sub, sub)
        g = gate_ref[t // per_seq, 0, pl.ds(off, sub), :]
        buf[slot, :, :D] = buf[slot, :, :D] + y * g
        write_back(t, slot)

    for t0 in range(FFN_AHEAD):
        lax.fori_loop(0, sub, lambda r, _, t0=t0: gather_row(t0, t0, r, 0), None, unroll=8)
    lax.fori_loop(0, n_tiles - FFN_AHEAD, lambda t, _: tile(t, True), None)
    for t in range(n_tiles - FFN_AHEAD, n_tiles):
        tile(t, False)
    for slot in range(FFN_SLOTS):
        wait_write_back(slot)


def expert_ffn(idx, gate, hx, w1, w3, w2, seq_len):
    T, D2 = hx.shape
    B, E, cap, _ = gate.shape
    D, F = w1.shape[1:]
    sub = min(256, cap)
    assert D2 == D + D // 2 and B * cap // sub >= FFN_SLOTS and cap % 1024 == 0
    return pl.pallas_call(
        functools.partial(_expert_ffn_kernel, sub=sub, seq_len=seq_len),
        grid=(E,),
        in_specs=[pl.BlockSpec(memory_space=pl.ANY),
                  pl.BlockSpec((B, 1, cap, 1), lambda e: (0, e, 0, 0)),
                  pl.BlockSpec(memory_space=pl.ANY),
                  pl.BlockSpec((1, D, F), lambda e: (e, 0, 0)),
                  pl.BlockSpec((1, D, F), lambda e: (e, 0, 0)),
                  pl.BlockSpec((1, F, D), lambda e: (e, 0, 0))],
        out_specs=pl.BlockSpec(memory_space=pl.ANY),
        out_shape=jax.ShapeDtypeStruct((T, D2), F32),
        scratch_shapes=[pltpu.SMEM((B * cap,), jnp.int32),
                        pltpu.VMEM((FFN_SLOTS, sub, D2), F32),
                        pltpu.SemaphoreType.DMA((1 + 2 * FFN_SLOTS,))],
        input_output_aliases={2: 0},
        compiler_params=_params("arbitrary"),
        name="expert_ffn",
    )(idx.reshape(B * E * cap), gate, hx, w1, w3, w2)


def _rope_tables(positions, dim, reps):
    inv = ROPE_THETA ** (-jnp.arange(0, dim, 2, dtype=F32) / dim)
    ang = positions.astype(F32).reshape(-1, 1) * inv
    cos, sin = jnp.cos(ang), jnp.sin(ang)
    return (jnp.tile(jnp.concatenate([cos, cos], axis=-1), (1, reps)),
            jnp.tile(jnp.concatenate([-sin, sin], axis=-1), (1, reps)))


def _by_class(a, d):
    B, S = a.shape[:2]
    return jnp.swapaxes(a.reshape(B, S // d, d, *a.shape[2:]), 1, 2)


def kernel(x, mem, positions, norm_mix, w_in, da_lambda, da_subln, conv_w, w_branch, w_gate, b_gate,
           w_out, norm_cross, norm_mem, wq_x, wkv_x, wo_x, norm_moe, w_router, w_e1, w_e3, w_e2,
           norm_final):
    B, S, D = x.shape
    depth = w_in.shape[0]
    E = w_router.shape[-1]
    cap = CAPACITY_FACTOR * S // E
    T = B * S

    ca, sa = _rope_tables(positions, DA_QK_DIM, LANES // DA_QK_DIM)
    tabs_b = [_rope_tables(_by_class(positions, d), DIL_HEAD_DIM, 1) for d in DILATIONS]
    kv_all = memory_kv(mem, norm_mem, wkv_x.astype(BF16))

    hx = x
    for l in range(depth):
        lam_init = jnp.full((1, 1), 0.8 - 0.6 * math.exp(-0.3 * l), F32)
        w_in_l = w_in[l].astype(BF16)
        xn, *xn_cls = rmsnorm_with_classes(hx, norm_mix[l])
        xn = xn.reshape(T, D)
        proj3 = in_projection(xn, w_in_l, ca, sa, *tabs_b[0]).reshape(B, S, NAT_COLS)
        ya = diff_attention(proj3, da_lambda[l], lam_init, da_subln[l]).reshape(T, A_WIDTH)
        outs, lses = [], []
        for g, (window, d) in enumerate(DIL_PAIRS):
            if d == 1:
                qkv, first = proj3.reshape(B, 1, S, NAT_COLS), B_FIRST_TILE
            else:
                xg = xn_cls[[dd for dd in DILATIONS if dd > 1].index(d)].reshape(T, D)
                qkv = in_projection(xg, w_in_l, ca, sa, *tabs_b[g], group=g)
                qkv, first = qkv.reshape(B, d, S // d, B_TILES_PER_GROUP * COL_TILE), 0
            o, lse = dilated_attention(qkv, first, window, d, g)
            outs.append(o)
            lses.append(lse)
        yb = merge_groups(outs, lses).reshape(T, B_WIDTH)
        yc = short_conv_branch(proj3, conv_w[l]).reshape(T, C_WIDTH)
        merged = gate_merge(xn, ya, yb, yc, w_gate[l].astype(BF16), b_gate[l].reshape(1, -1),
                            w_branch[l].astype(BF16))
        h = matmul_residual(merged, w_out[l].astype(BF16), hx.reshape(T, -1))
        hx, aff_t = cross_attention_router(
            h.reshape(B, S, D), norm_cross[l], wq_x[l].astype(BF16), kv_all[l],
            wo_x[l].astype(BF16), norm_moe[l], w_router[l].T)
        idx, gate = expert_choice_select(aff_t, cap)
        hx = expert_ffn(idx, gate, hx.reshape(T, -1), w_e1[l].astype(BF16),
                        w_e3[l].astype(BF16), w_e2[l].astype(BF16), S).reshape(B, S, -1)
    return rmsnorm(hx.reshape(T, -1), norm_final, F32).reshape(B, S, D)
```

```python
import functools
import math

import jax
import jax.numpy as jnp
from jax import lax
from jax.experimental import pallas as pl
from jax.experimental.pallas import tpu as pltpu

F32 = jnp.float32
BF16 = jnp.bfloat16

EPS = 1e-6
ROPE_THETA = 10000.0
DA_HEADS = 4
DA_QK_DIM = 64
DA_V_DIM = 2 * DA_QK_DIM
DIL_PAIRS = ((128, 1), (512, 4), (2048, 16))
DIL_HEADS = 4
DIL_HEAD_DIM = 128
CONV_CH = 1024
A_WIDTH = DA_HEADS * DA_V_DIM
B_WIDTH = DIL_HEADS * DIL_HEAD_DIM
C_WIDTH = CONV_CH
MIX_WIDTH = A_WIDTH + B_WIDTH + C_WIDTH
N_BRANCH = 3
IN_COLS = 3 * A_WIDTH + len(DIL_PAIRS) * 3 * B_WIDTH + 3 * CONV_CH
X_HEADS = 4
X_HEAD_DIM = 128
CAPACITY_FACTOR = 2

LANES = 128
V7X_VMEM_BYTES = 64 * 1024 * 1024
VMEM_LIMIT = V7X_VMEM_BYTES - 8 * 1024 * 1024
COL_TILE = 512
BF16_ROWS = 16
DMA_PRIORITIES = 2
DIL_QBLK = 256
NEG_BIG = -1e30


def _params(*sem):
    return pltpu.CompilerParams(dimension_semantics=sem, vmem_limit_bytes=VMEM_LIMIT)


def _rms(x, g):
    return x * lax.rsqrt(jnp.mean(x * x, axis=-1, keepdims=True) + EPS) * g


def _dot(a, b):
    return jnp.dot(a, b, preferred_element_type=F32)


def _dot_nt(a, b, **kw):
    return lax.dot_general(a, b, (((1,), (1,)), ((), ())), preferred_element_type=F32, **kw)


def _rms_kernel(x_ref, g_ref, o_ref):
    o_ref[...] = _rms(x_ref[...], g_ref[...]).astype(o_ref.dtype)


def rmsnorm(x, g, out_dtype):
    T = x.shape[0]
    D = g.shape[0]
    tm = min(512, T)
    return pl.pallas_call(
        _rms_kernel,
        grid=(T // tm,),
        in_specs=[pl.BlockSpec((tm, D), lambda i: (i, 0)),
                  pl.BlockSpec((1, D), lambda i: (0, 0))],
        out_specs=pl.BlockSpec((tm, D), lambda i: (i, 0)),
        out_shape=jax.ShapeDtypeStruct((T, D), out_dtype),
        compiler_params=_params("parallel"),
        name="rmsnorm",
    )(x, g.reshape(1, D))


DILATIONS = tuple(d for _, d in DIL_PAIRS)


def _rms_classes_kernel(x_ref, g_ref, o_ref, *rest):
    class_refs, scr = rest[:-1], rest[-1]
    y = _rms(x_ref[0], g_ref[...])
    o_ref[0] = y.astype(o_ref.dtype)
    tm, D = y.shape
    for c in range(D // LANES):
        cs = slice(c * LANES, (c + 1) * LANES)
        scr[c] = y[:, cs]
        for ref, d in zip(class_refs, [d for d in DILATIONS if d > 1]):
            for r in range(d):
                ref[0, r, :, cs] = scr[c, pl.ds(r, tm // d, stride=d), :].astype(ref.dtype)


def rmsnorm_with_classes(x3, g):
    B, S, _ = x3.shape
    D = g.shape[0]
    tm = min(512, S)
    ds = [d for d in DILATIONS if d > 1]
    return pl.pallas_call(
        _rms_classes_kernel,
        grid=(B, S // tm),
        in_specs=[pl.BlockSpec((1, tm, D), lambda b, i: (b, i, 0)),
                  pl.BlockSpec((1, D), lambda b, i: (0, 0))],
        out_specs=[pl.BlockSpec((1, tm, D), lambda b, i: (b, i, 0))]
        + [pl.BlockSpec((1, d, tm // d, D), lambda b, i: (b, 0, i, 0)) for d in ds],
        out_shape=[jax.ShapeDtypeStruct((B, S, D), BF16)]
        + [jax.ShapeDtypeStruct((B, d, S // d, D), BF16) for d in ds],
        scratch_shapes=[pltpu.VMEM((D // LANES, tm, LANES), F32)],
        compiler_params=_params("parallel", "parallel"),
        name="rmsnorm_classes",
    )(x3, g.reshape(1, D))


N_COL_TILES = IN_COLS // COL_TILE
A_ROPE_TILES = 2 * A_WIDTH // COL_TILE
B_FIRST_TILE = 3 * A_WIDTH // COL_TILE
B_TILES_PER_GROUP = 3 * B_WIDTH // COL_TILE
B_ROPE_PER_GROUP = 2 * B_WIDTH // COL_TILE
C_FIRST_TILE = B_FIRST_TILE + len(DIL_PAIRS) * B_TILES_PER_GROUP


INPROJ_ROW_CHUNKS = 2


def _inproj_kernel(x_ref, w_ref, ca_ref, sa_ref, cb_ref, sb_ref, o_ref, *, col_map):
    j = col_map(pl.program_id(1))
    jb = j - B_FIRST_TILE
    is_a = j < A_ROPE_TILES
    is_b = (j >= B_FIRST_TILE) & (j < C_FIRST_TILE) & (jb % B_TILES_PER_GROUP < B_ROPE_PER_GROUP)
    tm = x_ref.shape[0]

    def by_row_chunks(epilogue):
        for r in range(INPROJ_ROW_CHUNKS):
            rows = slice(r * tm // INPROJ_ROW_CHUNKS, (r + 1) * tm // INPROJ_ROW_CHUNKS)
            acc = _dot(x_ref[rows, :], w_ref[...])
            for t in range(COL_TILE // LANES):
                cols = slice(t * LANES, (t + 1) * LANES)
                o_ref[rows, cols] = epilogue(acc[:, cols], rows).astype(o_ref.dtype)

    @pl.when(is_a)
    def _():
        lane = lax.broadcasted_iota(jnp.int32, (tm // INPROJ_ROW_CHUNKS, LANES), 1)
        first = (lane % DA_QK_DIM) < (DA_QK_DIM // 2)

        def rope(x, rows):
            rot = jnp.where(first, pltpu.roll(x, LANES - DA_QK_DIM // 2, 1),
                            pltpu.roll(x, DA_QK_DIM // 2, 1))
            return x * ca_ref[rows, :] + rot * sa_ref[rows, :]
        by_row_chunks(rope)

    @pl.when(is_b)
    def _():
        by_row_chunks(lambda x, rows: x * cb_ref[rows, :]
                      + pltpu.roll(x, DIL_HEAD_DIM // 2, 1) * sb_ref[rows, :])

    @pl.when(jnp.logical_not(is_a | is_b))
    def _():
        by_row_chunks(lambda x, rows: x)


NAT_TILES = N_COL_TILES - (len(DIL_PAIRS) - 1) * B_TILES_PER_GROUP
NAT_COLS = NAT_TILES * COL_TILE
NAT_C_FIRST_TILE = B_FIRST_TILE + B_TILES_PER_GROUP


def _natural_col_map(j):
    return j + jnp.where(j >= NAT_C_FIRST_TILE, (len(DIL_PAIRS) - 1) * B_TILES_PER_GROUP, 0)


def in_projection(xn, w, ca, sa, cb, sb, group=None):
    T, D = xn.shape
    tm = min(2048, T)
    if group is None:
        n_tiles, col_map = NAT_TILES, _natural_col_map
    else:
        first = B_FIRST_TILE + group * B_TILES_PER_GROUP
        n_tiles, col_map = B_TILES_PER_GROUP, lambda j: j + first
    tab = pl.BlockSpec((tm, LANES), lambda i, j: (i, 0))
    return pl.pallas_call(
        functools.partial(_inproj_kernel, col_map=col_map),
        grid=(T // tm, n_tiles),
        in_specs=[pl.BlockSpec((tm, D), lambda i, j: (i, 0)),
                  pl.BlockSpec((D, COL_TILE), lambda i, j: (0, col_map(j))),
                  tab, tab, tab, tab],
        out_specs=pl.BlockSpec((tm, COL_TILE), lambda i, j: (i, j)),
        out_shape=jax.ShapeDtypeStruct((T, n_tiles * COL_TILE), BF16),
        compiler_params=_params("parallel", "arbitrary"),
        name="in_projection" if group is None else f"in_projection_g{group}",
    )(xn, w, ca, sa, cb, sb)


KV_UNROLL = 8


def _diff_attn_kernel(q_ref, k_ref, v_ref, lam_ref, li_ref, g_ref, o_ref, vext_ref, *, tk):
    S = k_ref.shape[1]
    tq = q_ref.shape[1]

    @pl.when(pl.program_id(2) == 0)
    def _():
        vext_ref[:, :DA_V_DIM] = v_ref[0]
        vext_ref[:, DA_V_DIM:] = jnp.ones((S, DA_V_DIM), BF16)

    q = q_ref[0] * jnp.asarray(DA_QK_DIM ** -0.5, BF16)
    lane = lax.broadcasted_iota(jnp.int32, q.shape, 1)
    zero = jnp.zeros_like(q)
    q0 = jnp.where(lane < DA_QK_DIM, q, zero)
    q1 = jnp.where(lane >= DA_QK_DIM, q, zero)

    def step(qm, kj, vj, m, a):
        s = _dot_nt(qm, kj)
        m_new = jnp.maximum(m, jnp.max(s, axis=-1, keepdims=True))
        p = jnp.exp((s - m_new).astype(BF16))
        a = jnp.exp(m - m_new) * a + _dot(p, vj)
        return m_new, a

    def body(j, carry):
        m0, a0, m1, a1 = carry
        off = pl.multiple_of(j * tk, tk)
        kj = k_ref[0, pl.ds(off, tk), :]
        vj = vext_ref[pl.ds(off, tk), :]
        m0, a0 = step(q0, kj, vj, m0, a0)
        m1, a1 = step(q1, kj, vj, m1, a1)
        return m0, a0, m1, a1

    mi = jnp.full((tq, 1), NEG_BIG, F32)
    ai = jnp.zeros((tq, 2 * DA_V_DIM), F32)
    _, a0, _, a1 = lax.fori_loop(0, S // tk, body, (mi, ai, mi, ai), unroll=KV_UNROLL)

    lp = lam_ref[...]
    lam_init = li_ref[...]
    lam = (jnp.exp(jnp.sum(lp[0:1] * lp[1:2], axis=-1, keepdims=True))
           - jnp.exp(jnp.sum(lp[2:3] * lp[3:4], axis=-1, keepdims=True)) + lam_init)
    o = a0[:, :DA_V_DIM] / a0[:, DA_V_DIM:] - lam * (a1[:, :DA_V_DIM] / a1[:, DA_V_DIM:])
    o_ref[0] = (_rms(o, g_ref[...]) * (1.0 - lam_init)).astype(o_ref.dtype)


def diff_attention(proj3, lam_p, lam_init, sub_gain):
    B, S, _ = proj3.shape
    tq = min(1024, S)
    tk = min(512, S)
    assert (S // tk) % KV_UNROLL == 0
    kb = A_WIDTH // DA_V_DIM
    return pl.pallas_call(
        functools.partial(_diff_attn_kernel, tk=tk),
        grid=(B, DA_HEADS, S // tq),
        in_specs=[pl.BlockSpec((1, tq, DA_V_DIM), lambda b, h, i: (b, i, h)),
                  pl.BlockSpec((1, S, DA_V_DIM), lambda b, h, i: (b, 0, kb + h)),
                  pl.BlockSpec((1, S, DA_V_DIM), lambda b, h, i: (b, 0, 2 * kb + h)),
                  pl.BlockSpec((4, DA_QK_DIM), lambda b, h, i: (0, 0)),
                  pl.BlockSpec((1, 1), lambda b, h, i: (0, 0)),
                  pl.BlockSpec((1, DA_V_DIM), lambda b, h, i: (0, 0))],
        out_specs=pl.BlockSpec((1, tq, DA_V_DIM), lambda b, h, i: (b, i, h)),
        out_shape=jax.ShapeDtypeStruct((B, S, A_WIDTH), BF16),
        scratch_shapes=[pltpu.VMEM((S, 2 * DA_V_DIM), BF16)],
        compiler_params=_params("parallel", "parallel", "arbitrary"),
        name="diff_attention",
    )(proj3, proj3, proj3, lam_p, lam_init, sub_gain.reshape(1, DA_V_DIM))


def _dilated_kernel(q_ref, k_ref, v_ref, o_ref, lse_ref, *, n_side):
    i = pl.program_id(2)
    tq = q_ref.shape[2]
    L = k_ref.shape[2]
    W = tq + 2 * n_side
    start = pl.multiple_of(jnp.clip(i * tq - n_side, 0, L - W), n_side)
    qpos = i * tq + lax.broadcasted_iota(jnp.int32, (tq, W), 0)
    kpos = start + lax.broadcasted_iota(jnp.int32, (tq, W), 1)
    valid = jnp.abs(kpos - qpos) <= n_side
    scale = DIL_HEAD_DIM ** -0.5
    for h in range(DIL_HEADS):
        hs = slice(h * DIL_HEAD_DIM, (h + 1) * DIL_HEAD_DIM)
        kw = k_ref[0, 0, pl.ds(start, W), hs]
        vw = v_ref[0, 0, pl.ds(start, W), hs]
        s = jnp.where(valid, _dot_nt(q_ref[0, 0, :, hs], kw) * scale, NEG_BIG)
        m = jnp.max(s, axis=-1, keepdims=True)
        p = jnp.exp((s - m).astype(BF16))
        ol = _dot(p, jnp.concatenate([vw, jnp.ones_like(vw)], axis=1))
        l = ol[:, DIL_HEAD_DIM:]
        o_ref[0, 0, :, hs] = ol[:, :DIL_HEAD_DIM] / l
        lse_ref[0, 0, :, hs] = m + jnp.log(l)


def dilated_attention(qkv, first_tile, window, dilation, group):
    B, d, L, _ = qkv.shape
    n_side = (window // 2) // dilation
    tq = next(t for t in (DIL_QBLK, DIL_QBLK // 2) if t + 2 * n_side <= L and L % t == 0)
    assert d == dilation and n_side % BF16_ROWS == 0

    def resident(kind):
        return pl.BlockSpec((1, 1, L, B_WIDTH), lambda b, r, i: (b, r, 0, first_tile + kind))

    out_spec = pl.BlockSpec((1, 1, tq, B_WIDTH), lambda b, r, i: (b, r, i, 0))
    return pl.pallas_call(
        functools.partial(_dilated_kernel, n_side=n_side),
        grid=(B, d, L // tq),
        in_specs=[pl.BlockSpec((1, 1, tq, B_WIDTH), lambda b, r, i: (b, r, i, first_tile)),
                  resident(1), resident(2)],
        out_specs=[out_spec, out_spec],
        out_shape=[jax.ShapeDtypeStruct((B, d, L, B_WIDTH), F32)] * 2,
        compiler_params=_params("parallel", "parallel", "arbitrary"),
        name=f"dilated_attention_g{group}",
    )(qkv, qkv, qkv)


def _merge_kernel(*refs):
    n = len(DILATIONS)
    ins, y_ref, scr = refs[:2 * n], refs[2 * n], refs[2 * n + 1:]
    tm = y_ref.shape[1]
    for c in range(B_WIDTH // LANES):
        cs = slice(c * LANES, (c + 1) * LANES)
        vals, k = [], 0
        for g, d in enumerate(DILATIONS):
            o_ref, l_ref = ins[2 * g], ins[2 * g + 1]
            if d == 1:
                vals.append((o_ref[0, 0, :, cs], l_ref[0, 0, :, cs]))
                continue
            so, sl = scr[2 * k], scr[2 * k + 1]
            k += 1
            for r in range(d):
                so[c, pl.ds(r, tm // d, stride=d), :] = o_ref[0, r, :, cs]
                sl[c, pl.ds(r, tm // d, stride=d), :] = l_ref[0, r, :, cs]
            vals.append((so[c], sl[c]))
        m = functools.reduce(jnp.maximum, [l for _, l in vals])
        ws = [jnp.exp(l - m) for _, l in vals]
        num = sum(w * o for w, (o, _) in zip(ws, vals))
        y_ref[0, :, cs] = (num / sum(ws)).astype(y_ref.dtype)


def merge_groups(outs, lses):
    B, _, S, _ = outs[0].shape
    tm = min(512, S)
    specs, args = [], []
    for d, o, l in zip(DILATIONS, outs, lses):
        sp = pl.BlockSpec((1, d, tm // d, B_WIDTH), lambda b, i: (b, 0, i, 0))
        specs += [sp, sp]
        args += [o, l]
    n_scr = 2 * sum(1 for d in DILATIONS if d > 1)
    return pl.pallas_call(
        _merge_kernel,
        grid=(B, S // tm),
        in_specs=specs,
        out_specs=pl.BlockSpec((1, tm, B_WIDTH), lambda b, i: (b, i, 0)),
        out_shape=jax.ShapeDtypeStruct((B, S, B_WIDTH), BF16),
        scratch_shapes=[pltpu.VMEM((B_WIDTH // LANES, tm, LANES), F32)] * n_scr,
        compiler_params=_params("parallel", "parallel"),
        name="merge_groups",
    )(*args)


HALO_ROWS = 16


def _conv_kernel(h_ref, gb_ref, gc_ref, hp_ref, gp_ref, hn_ref, gn_ref, w_ref, o_ref):
    i = pl.program_id(1)
    ts = h_ref.shape[1]
    u = gc_ref[0].astype(F32) * h_ref[0].astype(F32)
    up = (gp_ref[0].astype(F32) * hp_ref[0].astype(F32))[HALO_ROWS - 1:HALO_ROWS]
    un = (gn_ref[0].astype(F32) * hn_ref[0].astype(F32))[0:1]
    up = jnp.where(i > 0, up, 0.0)
    un = jnp.where(i < pl.num_programs(1) - 1, un, 0.0)
    rows = lax.broadcasted_iota(jnp.int32, u.shape, 0)
    u_prev = jnp.where(rows == 0, up, pltpu.roll(u, 1, 0))
    u_next = jnp.where(rows == ts - 1, un, pltpu.roll(u, ts - 1, 0))
    w = w_ref[...]
    y = gb_ref[0].astype(F32) * (w[0:1] * u_prev + w[1:2] * u + w[2:3] * u_next)
    o_ref[0] = y.astype(o_ref.dtype)


def short_conv_branch(proj3, conv_w):
    B, S, cols = proj3.shape
    ts = min(512, S)
    c0 = (cols - 3 * CONV_CH) // CONV_CH
    hb = ts // HALO_ROWS
    n_halo = S // HALO_ROWS

    def main(c):
        return pl.BlockSpec((1, ts, CONV_CH), lambda b, i: (b, i, c))

    def prev(c):
        return pl.BlockSpec((1, HALO_ROWS, CONV_CH), lambda b, i: (b, jnp.maximum(i * hb - 1, 0), c))

    def nxt(c):
        return pl.BlockSpec((1, HALO_ROWS, CONV_CH),
                            lambda b, i: (b, jnp.minimum((i + 1) * hb, n_halo - 1), c))

    return pl.pallas_call(
        _conv_kernel,
        grid=(B, S // ts),
        in_specs=[main(c0), main(c0 + 1), main(c0 + 2), prev(c0), prev(c0 + 2), nxt(c0), nxt(c0 + 2),
                  pl.BlockSpec((3, CONV_CH), lambda b, i: (0, 0))],
        out_specs=pl.BlockSpec((1, ts, CONV_CH), lambda b, i: (b, i, 0)),
        out_shape=jax.ShapeDtypeStruct((B, S, CONV_CH), BF16),
        compiler_params=_params("parallel", "parallel"),
        name="short_conv",
    )(proj3, proj3, proj3, proj3, proj3, proj3, proj3, conv_w)


def _gate_merge_kernel(xn_ref, ya_ref, yb_ref, yc_ref, wg0_ref, wg1_ref, wg2_ref,
                       b0_ref, b1_ref, b2_ref, wa_ref, wb_ref, wc_ref, o_ref):
    xn = xn_ref[...]
    acc = jax.nn.sigmoid(_dot(xn, wg0_ref[...]) + b0_ref[...]) * _dot(ya_ref[...], wa_ref[...])
    acc += jax.nn.sigmoid(_dot(xn, wg1_ref[...]) + b1_ref[...]) * _dot(yb_ref[...], wb_ref[...])
    acc += jax.nn.sigmoid(_dot(xn, wg2_ref[...]) + b2_ref[...]) * _dot(yc_ref[...], wc_ref[...])
    o_ref[...] = acc.astype(o_ref.dtype)


def gate_merge(xn, ya, yb, yc, w_gate, b_gate, w_br):
    T, D = xn.shape
    tm = min(1024, T)
    tn = min(512, D)
    nj = D // tn
    assert A_WIDTH == B_WIDTH and C_WIDTH == 2 * A_WIDTH

    def wg(k):
        return pl.BlockSpec((D, tn), lambda i, j: (0, k * nj + j))

    def bg(k):
        return pl.BlockSpec((1, tn), lambda i, j: (0, k * nj + j))

    def rowblk(w):
        return pl.BlockSpec((tm, w), lambda i, j: (i, 0))

    return pl.pallas_call(
        _gate_merge_kernel,
        grid=(T // tm, nj),
        in_specs=[rowblk(D), rowblk(A_WIDTH), rowblk(B_WIDTH), rowblk(C_WIDTH),
                  wg(0), wg(1), wg(2), bg(0), bg(1), bg(2),
                  pl.BlockSpec((A_WIDTH, tn), lambda i, j: (0, j)),
                  pl.BlockSpec((B_WIDTH, tn), lambda i, j: (1, j)),
                  pl.BlockSpec((C_WIDTH, tn), lambda i, j: (1, j))],
        out_specs=pl.BlockSpec((tm, tn), lambda i, j: (i, j)),
        out_shape=jax.ShapeDtypeStruct((T, D), BF16),
        compiler_params=_params("parallel", "arbitrary"),
        name="gate_merge",
    )(xn, ya, yb, yc, w_gate, w_gate, w_gate, b_gate, b_gate, b_gate, w_br, w_br, w_br)


def _matmul_residual_kernel(x_ref, w_ref, r_ref, o_ref):
    o_ref[...] = r_ref[...] + _dot(x_ref[...], w_ref[...])


def matmul_residual(x, w, res):
    T, K = x.shape
    N = w.shape[1]
    tm = min(512, T)
    tn = min(2048, N)
    return pl.pallas_call(
        _matmul_residual_kernel,
        grid=(T // tm, N // tn),
        in_specs=[pl.BlockSpec((tm, K), lambda i, j: (i, 0)),
                  pl.BlockSpec((K, tn), lambda i, j: (0, j)),
                  pl.BlockSpec((tm, tn), lambda i, j: (i, j))],
        out_specs=pl.BlockSpec((tm, tn), lambda i, j: (i, j)),
        out_shape=jax.ShapeDtypeStruct((T, N), F32),
        compiler_params=_params("parallel", "arbitrary"),
        name="matmul_residual",
    )(x, w, res)


def _memkv_kernel(mem_ref, g_ref, w_ref, o_ref):
    o_ref[0, 0] = _dot(_rms(mem_ref[0], g_ref[0]).astype(BF16), w_ref[0]).astype(o_ref.dtype)


def memory_kv(mem, norm_mem, wkv):
    B, M, D = mem.shape
    L, _, N = wkv.shape
    return pl.pallas_call(
        _memkv_kernel,
        grid=(L, B),
        in_specs=[pl.BlockSpec((1, M, D), lambda l, b: (b, 0, 0)),
                  pl.BlockSpec((1, 1, D), lambda l, b: (l, 0, 0)),
                  pl.BlockSpec((1, D, N), lambda l, b: (l, 0, 0))],
        out_specs=pl.BlockSpec((1, 1, M, N), lambda l, b: (l, b, 0, 0)),
        out_shape=jax.ShapeDtypeStruct((L, B, M, N), BF16),
        compiler_params=_params("parallel", "parallel"),
        name="memory_kv",
    )(mem, norm_mem.reshape(L, 1, D), wkv)


def _pack_bf16_pairs(x):
    m = x.shape[1] // 2
    hi = pltpu.bitcast(x[:, :m].astype(BF16).astype(F32), jnp.uint32)
    lo = pltpu.bitcast(x[:, m:].astype(BF16).astype(F32), jnp.uint32)
    return pltpu.bitcast(hi | (lo >> 16), F32)


def _unpack_bf16_pairs(w):
    bits = pltpu.bitcast(w, jnp.uint32)
    hi = pltpu.bitcast(bits & jnp.uint32(0xFFFF0000), F32).astype(BF16)
    lo = pltpu.bitcast(bits << 16, F32).astype(BF16)
    return jnp.concatenate([hi, lo], axis=1)


def _cross_kernel(h_ref, gx_ref, wq_ref, kv_ref, wo_ref, gm_ref, wr_ref, hx_ref, aff_ref):
    h = h_ref[0]
    D = h.shape[1]
    q = _dot(_rms(h, gx_ref[...]).astype(BF16), wq_ref[...]) * (X_HEAD_DIM ** -0.5)
    q = q.astype(BF16)
    kv = kv_ref[0]
    outs = []
    for hd in range(X_HEADS):
        ks = slice(hd * X_HEAD_DIM, (hd + 1) * X_HEAD_DIM)
        vs = slice((X_HEADS + hd) * X_HEAD_DIM, (X_HEADS + hd + 1) * X_HEAD_DIM)
        s = _dot_nt(q[:, ks], kv[:, ks])
        e = jnp.exp(s - jnp.max(s, axis=-1, keepdims=True))
        p = e / jnp.sum(e, axis=-1, keepdims=True)
        outs.append(_dot(p.astype(BF16), kv[:, vs]))
    o = jnp.concatenate(outs, axis=1).astype(BF16)
    hn = h + _dot(o, wo_ref[...])
    xn = _rms(hn, gm_ref[...])
    hx_ref[0, :, :D] = hn
    hx_ref[0, :, D:] = _pack_bf16_pairs(xn)
    wr = wr_ref[...]
    w_hi, x_hi = wr.astype(BF16), xn.astype(BF16)
    w_lo, x_lo = (wr - w_hi.astype(F32)).astype(BF16), (xn - x_hi.astype(F32)).astype(BF16)
    lt = _dot_nt(w_hi, x_hi) + (_dot_nt(w_hi, x_lo) + _dot_nt(w_lo, x_hi))
    e = jnp.exp(lt - jnp.max(lt, axis=0, keepdims=True))
    aff_ref[0] = e / jnp.sum(e, axis=0, keepdims=True)


def cross_attention_router(h3, g_cross, wq, kv, wo, g_moe, w_router_t):
    B, S, D = h3.shape
    M, NKV = kv.shape[1:]
    E = w_router_t.shape[0]
    NQ = wq.shape[1]
    tm = min(512, S)

    def full(shape):
        return pl.BlockSpec(shape, lambda b, i: (0,) * len(shape))

    return pl.pallas_call(
        _cross_kernel,
        grid=(B, S // tm),
        in_specs=[pl.BlockSpec((1, tm, D), lambda b, i: (b, i, 0)),
                  full((1, D)), full((D, NQ)),
                  pl.BlockSpec((1, M, NKV), lambda b, i: (b, 0, 0)),
                  full((NQ, D)), full((1, D)), full((E, D))],
        out_specs=[pl.BlockSpec((1, tm, D + D // 2), lambda b, i: (b, i, 0)),
                   pl.BlockSpec((1, E, tm), lambda b, i: (b, 0, i))],
        out_shape=[jax.ShapeDtypeStruct((B, S, D + D // 2), F32),
                   jax.ShapeDtypeStruct((B, E, S), F32)],
        compiler_params=_params("parallel", "parallel"),
        name="cross_attention_router",
    )(h3, g_cross.reshape(1, D), wq, kv, wo, g_moe.reshape(1, D), w_router_t)


SLOT_BLK = 128


def _select_kernel(aff_ref, cs_ref, w_ref, ts_ref, first_ref, *, cap):
    E, C, _ = aff_ref.shape[1:]
    R = E * C
    a = aff_ref[0].reshape(R, LANES)
    bits = pltpu.bitcast(a, jnp.int32)
    rid = lax.broadcasted_iota(jnp.int32, (R, LANES), 0)

    ii = lax.broadcasted_iota(jnp.int32, (LANES, LANES), 0)
    jj = lax.broadcasted_iota(jnp.int32, (LANES, LANES), 1)
    upper = jnp.where(ii <= jj, 1.0, 0.0).astype(BF16)
    ones = jnp.ones((LANES, LANES), BF16)
    ri = lax.broadcasted_iota(jnp.int32, (R, R), 0) // C
    rj = lax.broadcasted_iota(jnp.int32, (R, R), 1)
    same_expert = jnp.where(rj // C == ri, 1.0, 0.0).astype(BF16)
    rows_before = jnp.where((rj // C == ri) & (rj < lax.broadcasted_iota(jnp.int32, (R, R), 0)),
                            1.0, 0.0).astype(BF16)

    def row_totals(x):
        return _dot(x.astype(BF16), ones).astype(BF16)

    def count(x):
        return _dot(same_expert, row_totals(x))

    def prefix_count(x):
        return _dot(x.astype(BF16), upper) + _dot(rows_before, row_totals(x))

    thr = jnp.zeros((R, LANES), jnp.int32)
    for bit in range(30, -1, -1):
        cand = thr | jnp.int32(1 << bit)
        thr = jnp.where(count(jnp.where(bits >= cand, 1.0, 0.0)) >= cap, cand, thr)
    gt_f = jnp.where(bits > thr, 1.0, 0.0)
    eq_f = jnp.where(bits == thr, 1.0, 0.0)

    need = cap - count(gt_f)
    eq_rank = prefix_count(eq_f) - eq_f
    sel = gt_f + eq_f * jnp.where(eq_rank < need, 1.0, 0.0)
    csum = prefix_count(sel)
    tok = ((rid % C) * LANES + lax.broadcasted_iota(jnp.int32, (R, LANES), 1)).astype(F32)
    cs_ref[0] = csum
    w_ref[0] = a * sel
    ts_ref[0] = tok * sel
    before = _dot(rows_before, row_totals(sel))
    for blk in range(R // LANES):
        d = jnp.where(ii == jj, before[blk * LANES:(blk + 1) * LANES], 0.0)
        first_ref[0, blk:blk + 1, :] = jnp.sum(d, axis=0, keepdims=True).astype(jnp.int32)


def _resolve_kernel(first_ref, cs_ref, w_ref, ts_ref, idx_ref, gate_ref, acc_i, acc_g, *, cap):
    b = pl.program_id(0)
    e = pl.program_id(1)
    C = cs_ref.shape[1]
    n_blk = cap // SLOT_BLK
    acc_i[...] = jnp.zeros_like(acc_i)
    acc_g[...] = jnp.zeros_like(acc_g)

    def chunk(c, _):
        p0 = first_ref[b, e * C + c]
        p1 = jnp.where(c + 1 < C, first_ref[b, e * C + jnp.minimum(c + 1, C - 1)], cap)
        cs_row = cs_ref[0, pl.ds(c, 1), :]
        ts_row = ts_ref[0, pl.ds(c, 1), :]
        w_row = w_ref[0, pl.ds(c, 1), :]
        for j in range(LANES // SLOT_BLK + 1):
            blk = p0 // SLOT_BLK + j

            @pl.when((blk * SLOT_BLK < p1) & (blk < n_blk))
            def _():
                off = pl.multiple_of(blk * SLOT_BLK, SLOT_BLK)
                k1 = (lax.broadcasted_iota(jnp.int32, (SLOT_BLK, LANES), 0) + (off + 1)).astype(F32)
                hit = cs_row == k1
                rows = pl.ds(off, SLOT_BLK)
                acc_i[rows, :] = acc_i[rows, :] + jnp.where(hit, ts_row, 0.0)
                acc_g[rows, :] = acc_g[rows, :] + jnp.where(hit, w_row, 0.0)
        return 0

    lax.fori_loop(0, C, chunk, 0)
    idx_ref[0, 0] = jnp.sum(acc_i[...], axis=1, keepdims=True).astype(jnp.int32)
    gate_ref[0, 0] = jnp.sum(acc_g[...], axis=1, keepdims=True)


def expert_choice_select(aff_t, cap):
    B, E, S = aff_t.shape
    C = S // LANES
    R = E * C
    assert cap % SLOT_BLK == 0 and R % LANES == 0
    rows = pl.BlockSpec((1, R, LANES), lambda b: (b, 0, 0))
    cs, w, ts, first = pl.pallas_call(
        functools.partial(_select_kernel, cap=cap),
        grid=(B,),
        in_specs=[pl.BlockSpec((1, E, C, LANES), lambda b: (b, 0, 0, 0))],
        out_specs=[rows, rows, rows, pl.BlockSpec((1, R // LANES, LANES), lambda b: (b, 0, 0))],
        out_shape=[jax.ShapeDtypeStruct((B, R, LANES), F32)] * 3
        + [jax.ShapeDtypeStruct((B, R // LANES, LANES), jnp.int32)],
        compiler_params=_params("parallel"),
        name="expert_choice_select",
    )(aff_t.reshape(B, E, C, LANES))

    chunk_rows = pl.BlockSpec((1, C, LANES), lambda b, e, first: (b * E + e, 0, 0))
    slots = pl.BlockSpec((1, 1, cap, 1), lambda b, e, first: (b, e, 0, 0))
    return pl.pallas_call(
        functools.partial(_resolve_kernel, cap=cap),
        grid_spec=pltpu.PrefetchScalarGridSpec(
            num_scalar_prefetch=1,
            grid=(B, E),
            in_specs=[chunk_rows] * 3,
            out_specs=[slots, slots],
            scratch_shapes=[pltpu.VMEM((cap, LANES), F32)] * 2),
        out_shape=[jax.ShapeDtypeStruct((B, E, cap, 1), jnp.int32),
                   jax.ShapeDtypeStruct((B, E, cap, 1), F32)],
        compiler_params=_params("parallel", "parallel"),
        name="expert_choice_resolve",
    )(first.reshape(B, R), *[x.reshape(B * E, C, LANES) for x in (cs, w, ts)])


FFN_SLOTS = 4
FFN_AHEAD = 2


def _expert_ffn_kernel(idx_hbm, gate_ref, hx_in_hbm, w1_ref, w3_ref, w2_ref, hx_hbm,
                       idx_smem, buf, sems, *, sub, seq_len):
    del hx_in_hbm
    e = pl.program_id(0)
    n_exp = pl.num_programs(0)
    n_seq = gate_ref.shape[0]
    cap = idx_smem.shape[0] // n_seq
    D = w2_ref.shape[2]
    per_seq = cap // sub
    n_tiles = n_seq * per_seq
    for b in range(n_seq):
        cp = pltpu.make_async_copy(idx_hbm.at[pl.ds((b * n_exp + e) * cap, cap)],
                                   idx_smem.at[pl.ds(b * cap, cap)], sems.at[0])
        cp.start()
        cp.wait()

    def token_row(t, r):
        return pl.ds((t // per_seq) * seq_len + idx_smem[t * sub + r], 1)

    def gather_row(t, slot, r, priority):
        pltpu.make_async_copy(hx_hbm.at[token_row(t, r), :], buf.at[slot, pl.ds(r, 1), :],
                              sems.at[1 + slot]).start(priority=priority)

    def wait_gather(slot):
        pltpu.make_async_copy(hx_hbm.at[pl.ds(0, sub), :], buf.at[slot], sems.at[1 + slot]).wait()

    def write_back(t, slot):
        for r in range(sub):
            pltpu.make_async_copy(buf.at[slot, pl.ds(r, 1), pl.ds(0, D)],
                                  hx_hbm.at[token_row(t, r), pl.ds(0, D)],
                                  sems.at[1 + FFN_SLOTS + slot]).start(priority=r % DMA_PRIORITIES)

    def wait_write_back(slot):
        pltpu.make_async_copy(buf.at[slot, :, pl.ds(0, D)], hx_hbm.at[pl.ds(0, sub), pl.ds(0, D)],
                              sems.at[1 + FFN_SLOTS + slot]).wait()

    def tile(t, fetch_ahead):
        slot = t % FFN_SLOTS
        wait_gather(slot)
        if fetch_ahead:
            ahead = (t + FFN_AHEAD) % FFN_SLOTS

            @pl.when(t >= FFN_SLOTS - FFN_AHEAD)
            def _():
                wait_write_back(ahead)
            for r in range(sub):
                gather_row(t + FFN_AHEAD, ahead, r, r % DMA_PRIORITIES)
        xg = _unpack_bf16_pairs(buf[slot, :, D:])
        a = _dot(xg, w1_ref[0])
        c = _dot(xg, w3_ref[0])
        hid = (a * jax.nn.sigmoid(a) * c).astype(BF16)
        y = _dot(hid, w2_ref[0])
        off = pl.multiple_of((t % per_seq) * sub, sub)
        g = gate_ref[t // per_seq, 0, pl.ds(off, sub), :]
        buf[slot, :, :D] = buf[slot, :, :D] + y * g
        write_back(t, slot)

    for t0 in range(FFN_AHEAD):
        lax.fori_loop(0, sub, lambda r, _, t0=t0: gather_row(t0, t0, r, 0), None, unroll=8)
    lax.fori_loop(0, n_tiles - FFN_AHEAD, lambda t, _: tile(t, True), None)
    for t in range(n_tiles - FFN_AHEAD, n_tiles):
        tile(t, False)
    for slot in range(FFN_SLOTS):
        wait_write_back(slot)


def expert_ffn(idx, gate, hx, w1, w3, w2, seq_len):
    T, D2 = hx.shape
    B, E, cap, _ = gate.shape
    D, F = w1.shape[1:]
    sub = min(256, cap)
    assert D2 == D + D // 2 and B * cap // sub >= FFN_SLOTS and cap % 1024 == 0
    return pl.pallas_call(
        functools.partial(_expert_ffn_kernel, sub=sub, seq_len=seq_len),
        grid=(E,),
        in_specs=[pl.BlockSpec(memory_space=pl.ANY),
                  pl.BlockSpec((B, 1, cap, 1), lambda e: (0, e, 0, 0)),
                  pl.BlockSpec(memory_space=pl.ANY),
                  pl.BlockSpec((1, D, F), lambda e: (e, 0, 0)),
                  pl.BlockSpec((1, D, F), lambda e: (e, 0, 0)),
                  pl.BlockSpec((1, F, D), lambda e: (e, 0, 0))],
        out_specs=pl.BlockSpec(memory_space=pl.ANY),
        out_shape=jax.ShapeDtypeStruct((T, D2), F32),
        scratch_shapes=[pltpu.SMEM((B * cap,), jnp.int32),
                        pltpu.VMEM((FFN_SLOTS, sub, D2), F32),
                        pltpu.SemaphoreType.DMA((1 + 2 * FFN_SLOTS,))],
        input_output_aliases={2: 0},
        compiler_params=_params("arbitrary"),
        name="expert_ffn",
    )(idx.reshape(B * E * cap), gate, hx, w1, w3, w2)


def _rope_tables(positions, dim, reps):
    inv = ROPE_THETA ** (-jnp.arange(0, dim, 2, dtype=F32) / dim)
    ang = positions.astype(F32).reshape(-1, 1) * inv
    cos, sin = jnp.cos(ang), jnp.sin(ang)
    return (jnp.tile(jnp.concatenate([cos, cos], axis=-1), (1, reps)),
            jnp.tile(jnp.concatenate([-sin, sin], axis=-1), (1, reps)))


def _by_class(a, d):
    B, S = a.shape[:2]
    return jnp.swapaxes(a.reshape(B, S // d, d, *a.shape[2:]), 1, 2)


def kernel(x, mem, positions, norm_mix, w_in, da_lambda, da_subln, conv_w, w_branch, w_gate, b_gate,
           w_out, norm_cross, norm_mem, wq_x, wkv_x, wo_x, norm_moe, w_router, w_e1, w_e3, w_e2,
           norm_final):
    B, S, D = x.shape
    depth = w_in.shape[0]
    E = w_router.shape[-1]
    cap = CAPACITY_FACTOR * S // E
    T = B * S

    ca, sa = _rope_tables(positions, DA_QK_DIM, LANES // DA_QK_DIM)
    tabs_b = [_rope_tables(_by_class(positions, d), DIL_HEAD_DIM, 1) for d in DILATIONS]
    kv_all = memory_kv(mem, norm_mem, wkv_x.astype(BF16))

    hx = x
    for l in range(depth):
        lam_init = jnp.full((1, 1), 0.8 - 0.6 * math.exp(-0.3 * l), F32)
        w_in_l = w_in[l].astype(BF16)
        xn, *xn_cls = rmsnorm_with_classes(hx, norm_mix[l])
        xn = xn.reshape(T, D)
        proj3 = in_projection(xn, w_in_l, ca, sa, *tabs_b[0]).reshape(B, S, NAT_COLS)
        ya = diff_attention(proj3, da_lambda[l], lam_init, da_subln[l]).reshape(T, A_WIDTH)
        outs, lses = [], []
        for g, (window, d) in enumerate(DIL_PAIRS):
            if d == 1:
                qkv, first = proj3.reshape(B, 1, S, NAT_COLS), B_FIRST_TILE
            else:
                xg = xn_cls[[dd for dd in DILATIONS if dd > 1].index(d)].reshape(T, D)
                qkv = in_projection(xg, w_in_l, ca, sa, *tabs_b[g], group=g)
                qkv, first = qkv.reshape(B, d, S // d, B_TILES_PER_GROUP * COL_TILE), 0
            o, lse = dilated_attention(qkv, first, window, d, g)
            outs.append(o)
            lses.append(lse)
        yb = merge_groups(outs, lses).reshape(T, B_WIDTH)
        yc = short_conv_branch(proj3, conv_w[l]).reshape(T, C_WIDTH)
        merged = gate_merge(xn, ya, yb, yc, w_gate[l].astype(BF16), b_gate[l].reshape(1, -1),
                            w_branch[l].astype(BF16))
        h = matmul_residual(merged, w_out[l].astype(BF16), hx.reshape(T, -1))
        hx, aff_t = cross_attention_router(
            h.reshape(B, S, D), norm_cross[l], wq_x[l].astype(BF16), kv_all[l],
            wo_x[l].astype(BF16), norm_moe[l], w_router[l].T)
        idx, gate = expert_choice_select(aff_t, cap)
        hx = expert_ffn(idx, gate, hx.reshape(T, -1), w_e1[l].astype(BF16),
                        w_e3[l].astype(BF16), w_e2[l].astype(BF16), S).reshape(B, S, -1)
    return rmsnorm(hx.reshape(T, -1), norm_final, F32).reshape(B, S, D)
```
